```python
import jax, jax.numpy as jnp
from jax import lax
import numpy as np

D_MODEL = 1024
BATCH = 1
SEQ = 16384
DEPTH = 4

CHUNK = 64
RET_HEADS = 4
RET_QK_DIM = 256
RET_V_DIM = 512
RET_QK_W = RET_HEADS * RET_QK_DIM
RET_V_W = RET_HEADS * RET_V_DIM
ROPE_THETA = 10000.0
CONV_CH = D_MODEL
CONV_WIDTH = 31
MEM_LEN = 256
X_HEADS = 4
X_HEAD_DIM = D_MODEL // X_HEADS
FFN_DIM = 2816
FFN_CONV_WIDTH = 3
IN_SPLITS = (RET_QK_W, RET_QK_W, RET_V_W, RET_V_W, 2 * CONV_CH, 2 * D_MODEL)
IN_W = sum(IN_SPLITS)
RMS_EPS = 1e-6
LN_EPS = 1e-5

kernel_name = "hybrid_retention_conformer_stream_block"


def rms_norm(x, g):
    xf = x.astype(jnp.float32)
    y = xf * lax.rsqrt(jnp.mean(xf * xf, axis=-1, keepdims=True) + RMS_EPS)
    return (y * g.astype(jnp.float32)).astype(x.dtype)


def layer_norm(x, g, b):
    xf = x.astype(jnp.float32)
    mu = jnp.mean(xf, axis=-1, keepdims=True)
    var = jnp.mean(jnp.square(xf - mu), axis=-1, keepdims=True)
    y = (xf - mu) * lax.rsqrt(var + LN_EPS)
    return (y * g.astype(jnp.float32) + b.astype(jnp.float32)).astype(x.dtype)


def causal_dwconv(x, w, b):
    width = w.shape[0]
    y = lax.conv_general_dilated(
        x, w[:, None, :].astype(x.dtype), window_strides=(1,),
        padding=[(width - 1, 0)], dimension_numbers=("NWC", "WIO", "NWC"),
        feature_group_count=x.shape[-1])
    return y + b.astype(x.dtype)


def rotary(x, cos, sin):
    half = x.shape[-1] // 2
    x1, x2 = x[..., :half], x[..., half:]
    return jnp.concatenate([x1 * cos - x2 * sin, x2 * cos + x1 * sin], axis=-1)


def chunk_retention(q, k, v):
    bsz, seq, heads, dk = q.shape
    dv = v.shape[-1]
    n_chunks = seq // CHUNK
    dt = q.dtype
    qc = q.reshape(bsz, n_chunks, CHUNK, heads, dk)
    kc = k.reshape(bsz, n_chunks, CHUNK, heads, dk)
    vc = v.reshape(bsz, n_chunks, CHUNK, heads, dv)

    log_gamma = jnp.log(1.0 - jnp.power(2.0, -5.0 - jnp.arange(heads, dtype=jnp.float32)))
    idx = jnp.arange(CHUNK, dtype=jnp.float32)
    dist = jnp.abs(idx[:, None] - idx[None, :])
    d_inner = jnp.exp(log_gamma[:, None, None] * dist).astype(dt)
    decay_q = jnp.exp(log_gamma[None, :] * (idx[:, None] + 1.0)).astype(dt)
    decay_k = jnp.exp(log_gamma[None, :] * (CHUNK - 1.0 - idx[:, None])).astype(dt)
    decay_chunk = jnp.exp(log_gamma * CHUNK).astype(dt)

    scores = jnp.einsum("bnchd,bnkhd->bnhck", qc, kc) * d_inner
    o_inner = jnp.einsum("bnhck,bnkhv->bnchv", scores, vc)

    kc_dec = kc * decay_k[:, :, None]
    xs = (jnp.moveaxis(qc, 1, 0), jnp.moveaxis(kc_dec, 1, 0), jnp.moveaxis(vc, 1, 0))

    def step(state, inp):
        q_i, k_i, v_i = inp
        cross = jnp.einsum("bchk,bhkv->bchv", q_i, state)
        state = state * decay_chunk[None, :, None, None] + jnp.einsum("bchk,bchv->bhkv", k_i, v_i)
        return state, cross

    state0 = jnp.zeros((bsz, heads, dk, dv), dt)
    _, cross = lax.scan(step, state0, xs)
    o_cross = jnp.moveaxis(cross, 0, 1) * decay_q[:, :, None]
    return (o_inner + o_cross).reshape(bsz, seq, heads, dv)


def head_group_norm(o, g):
    of = o.astype(jnp.float32)
    mu = jnp.mean(of, axis=-1, keepdims=True)
    var = jnp.mean(jnp.square(of - mu), axis=-1, keepdims=True)
    y = (of - mu) * lax.rsqrt(var + LN_EPS)
    bsz, seq, heads, dv = o.shape
    return (y.reshape(bsz, seq, heads * dv) * g.astype(jnp.float32)).astype(o.dtype)


def mixer_sublayer(u, cos, sin, w_in, b_gate, ret_gn_g, w_ret_out, conv_dw_w, conv_dw_b,
                   conv_ln_g, conv_ln_b, w_conv_out, b_conv_out, w_mix_out):
    bsz, seq, _ = u.shape
    p = u @ w_in
    cuts = np.cumsum(IN_SPLITS)[:-1].tolist()
    q, k, v, g_ret, c_in, gates = jnp.split(p, cuts, axis=-1)

    q = rotary(q.reshape(bsz, seq, RET_HEADS, RET_QK_DIM), cos, sin) * (RET_QK_DIM ** -0.5)
    k = rotary(k.reshape(bsz, seq, RET_HEADS, RET_QK_DIM), cos, sin)
    v = v.reshape(bsz, seq, RET_HEADS, RET_V_DIM)
    o = head_group_norm(chunk_retention(q, k, v), ret_gn_g)
    y_a = (jax.nn.silu(g_ret) * o) @ w_ret_out

    a, b = jnp.split(c_in, 2, axis=-1)
    c = a * jax.nn.sigmoid(b)
    c = causal_dwconv(c, conv_dw_w, conv_dw_b)
    c = jax.nn.silu(layer_norm(c, conv_ln_g, conv_ln_b))
    y_b = c @ w_conv_out + b_conv_out

    g_a, g_b = jnp.split(jax.nn.sigmoid(gates + b_gate), 2, axis=-1)
    return (g_a * y_a + g_b * y_b) @ w_mix_out


def memory_cross_attention(h, mem_n, w_xq, w_xkv, w_xo):
    bsz, seq, _ = h.shape
    q = (h @ w_xq).reshape(bsz, seq, X_HEADS, X_HEAD_DIM)
    k, v = jnp.split(mem_n @ w_xkv, 2, axis=-1)
    k = k.reshape(bsz, MEM_LEN, X_HEADS, X_HEAD_DIM)
    v = v.reshape(bsz, MEM_LEN, X_HEADS, X_HEAD_DIM)
    s = jnp.einsum("bshd,bmhd->bhsm", q, k).astype(jnp.float32) * (X_HEAD_DIM ** -0.5)
    pr = jax.nn.softmax(s, axis=-1).astype(h.dtype)
    o = jnp.einsum("bhsm,bmhd->bshd", pr, v).reshape(bsz, seq, D_MODEL)
    return o @ w_xo


def conv_ffn(h, w_up, ffn_dw_w, ffn_dw_b, w_down):
    val, gate = jnp.split(h @ w_up, 2, axis=-1)
    gate = causal_dwconv(gate, ffn_dw_w, ffn_dw_b)
    return (jax.nn.silu(gate) * val) @ w_down


def setup_inputs(seed: int = 0) -> dict:
    key = jax.random.key(seed)
    ks = iter(jax.random.split(key, 32))
    f32 = jnp.float32
    L, D = DEPTH, D_MODEL

    def w(shape, fan_in):
        return jax.random.normal(next(ks), shape, f32) * (fan_in ** -0.5)

    def gain(shape):
        return 1.0 + 0.02 * jax.random.normal(next(ks), shape, f32)

    def bias(shape):
        return 0.02 * jax.random.normal(next(ks), shape, f32)

    x = jax.random.normal(next(ks), (BATCH, SEQ, D), f32)
    mem = jax.random.normal(next(ks), (BATCH, MEM_LEN, D), f32)
    positions = jnp.broadcast_to(jnp.arange(SEQ, dtype=jnp.int32)[None, :], (BATCH, SEQ))
    return {
        "x": x,
        "mem": mem,
        "positions": positions,
        "norm_mix_g": gain((L, D)),
        "w_in": w((L, D, IN_W), D),
        "b_gate": bias((L, 2 * D)),
        "ret_gn_g": gain((L, RET_V_W)),
        "w_ret_out": w((L, RET_V_W, D), RET_V_W),
        "conv_dw_w": w((L, CONV_WIDTH, CONV_CH), CONV_WIDTH),
        "conv_dw_b": bias((L, CONV_CH)),
        "conv_ln_g": gain((L, CONV_CH)),
        "conv_ln_b": bias((L, CONV_CH)),
        "w_conv_out": w((L, CONV_CH, D), CONV_CH),
        "b_conv_out": bias((L, D)),
        "w_mix_out": w((L, D, D), D),
        "norm_xattn_g": gain((L, D)),
        "norm_mem_g": gain((L, D)),
        "w_xq": w((L, D, D), D),
        "w_xkv": w((L, D, 2 * D), D),
        "w_xo": w((L, D, D), D),
        "norm_ffn_g": gain((L, D)),
        "w_up": w((L, D, 2 * FFN_DIM), D),
        "ffn_dw_w": w((L, FFN_CONV_WIDTH, FFN_DIM), FFN_CONV_WIDTH),
        "ffn_dw_b": bias((L, FFN_DIM)),
        "w_down": w((L, FFN_DIM, D), FFN_DIM),
        "norm_final_g": gain((D,)),
    }


def reference(x, mem, positions, norm_mix_g, w_in, b_gate, ret_gn_g, w_ret_out,
              conv_dw_w, conv_dw_b, conv_ln_g, conv_ln_b, w_conv_out, b_conv_out,
              w_mix_out, norm_xattn_g, norm_mem_g, w_xq, w_xkv, w_xo, norm_ffn_g,
              w_up, ffn_dw_w, ffn_dw_b, w_down, norm_final_g):
    inv_freq = 1.0 / (ROPE_THETA ** (jnp.arange(0, RET_QK_DIM, 2, dtype=jnp.float32) / RET_QK_DIM))
    ang = positions.astype(jnp.float32)[..., None] * inv_freq
    cos = jnp.cos(ang)[:, :, None, :].astype(x.dtype)
    sin = jnp.sin(ang)[:, :, None, :].astype(x.dtype)

    h = x
    for l in range(DEPTH):
        u = rms_norm(h, norm_mix_g[l])
        h = h + mixer_sublayer(u, cos, sin, w_in[l], b_gate[l], ret_gn_g[l], w_ret_out[l],
                               conv_dw_w[l], conv_dw_b[l], conv_ln_g[l], conv_ln_b[l],
                               w_conv_out[l], b_conv_out[l], w_mix_out[l])
        mem_n = rms_norm(mem, norm_mem_g[l])
        h = h + memory_cross_attention(rms_norm(h, norm_xattn_g[l]), mem_n,
                                       w_xq[l], w_xkv[l], w_xo[l])
        h = h + conv_ffn(rms_norm(h, norm_ffn_g[l]), w_up[l], ffn_dw_w[l], ffn_dw_b[l], w_down[l])
    return rms_norm(h, norm_final_g)
```

```python
import functools

import jax
import jax.numpy as jnp
from jax import lax
from jax.experimental import pallas as pl
from jax.experimental.pallas import tpu as pltpu

F32 = jnp.float32
BF16 = jnp.bfloat16

D_MODEL = 1024
CHUNK = 64
RET_HEADS = 4
RET_QK_DIM = 256
RET_V_DIM = 512
RET_QK_W = RET_HEADS * RET_QK_DIM
RET_V_W = RET_HEADS * RET_V_DIM
ROPE_THETA = 10000.0
CONV_CH = D_MODEL
CONV_WIDTH = 31
X_HEADS = 4
X_HEAD_DIM = D_MODEL // X_HEADS
FFN_DIM = 2816
FFN_CONV_WIDTH = 3
RMS_EPS = 1e-6
LN_EPS = 1e-5

OFF_Q = 0
OFF_K = OFF_Q + RET_QK_W
OFF_V = OFF_K + RET_QK_W
OFF_G = OFF_V + RET_V_W
OFF_CA = OFF_G + RET_V_W
OFF_CB = OFF_CA + CONV_CH
OFF_GT = OFF_CB + CONV_CH
IN_W = OFF_GT + 2 * D_MODEL

LANES = 128
SUBLANES = 8
VMEM_LIMIT = 56 * 1024 * 1024

ROPE_ROWS = 1024
INPROJ_ROWS = 256
RET_ROWS = 256
MIX_ROWS = 512
XATTN_ROWS = 512
FFN_ROWS = 256
FFN_COLS = 256
NORM_ROWS = 1024
CONV_HALO = 32
CONV_ROW_CHUNK = 64


def _resident(block_shape, index):
    return pl.BlockSpec(block_shape, lambda i: index, pipeline_mode=pl.Buffered(1))


def _rows(block_rows, width):
    return pl.BlockSpec((block_rows, width), lambda i: (i, 0))


def _params(semantics):
    return pltpu.CompilerParams(dimension_semantics=(semantics,), vmem_limit_bytes=VMEM_LIMIT)


def _rms_norm(x, g):
    ms = jnp.mean(x * x, axis=-1, keepdims=True)
    return x * lax.rsqrt(ms + RMS_EPS) * g


def _sigmoid(x):
    return 1.0 / (1.0 + jnp.exp(-x))


def _dot(a, b):
    return jnp.dot(a, b, preferred_element_type=F32)


def _dot_nt(a, b):
    return lax.dot_general(a, b, (((1,), (1,)), ((), ())), preferred_element_type=F32)


def _rope_body(pos_ref, invf_ref, cos_ref, sin_ref):
    ang = pos_ref[...] * invf_ref[...]
    cos_ref[...] = jnp.cos(ang)
    sin_ref[...] = jnp.sin(ang)


def _rope_tables(pos, inv_freq):
    seq = pos.shape[0]
    half = inv_freq.shape[1]
    return pl.pallas_call(
        _rope_body,
        out_shape=(jax.ShapeDtypeStruct((seq, half), F32),) * 2,
        grid=(seq // ROPE_ROWS,),
        in_specs=[_rows(ROPE_ROWS, 1), _resident((1, half), (0, 0))],
        out_specs=(_rows(ROPE_ROWS, half),) * 2,
        compiler_params=_params("parallel"),
        name="rope_tables",
    )(pos, inv_freq)


def _inproj_body(h_ref, ng_ref, w_ref, cos_ref, sin_ref, bg_ref, cw_ref, cb_ref, lg_ref, lb_ref,
                 q_ref, k_ref, v_ref, gs_ref, cs_ref, gt_ref, cpad_ref, y_ref):
    rows = h_ref.shape[0]
    half = RET_QK_DIM // 2

    @pl.when(pl.program_id(0) == 0)
    def _():
        cpad_ref[0:CONV_HALO, :] = jnp.zeros((CONV_HALO, CONV_CH), F32)

    u = _rms_norm(h_ref[...], ng_ref[...]).astype(BF16)
    cos = cos_ref[...]
    sin = sin_ref[...]

    def proj(col, width):
        return _dot(u, w_ref[:, col:col + width])

    for hd in range(RET_HEADS):
        for base, out_ref, scale in ((OFF_Q, q_ref, RET_QK_DIM ** -0.5), (OFF_K, k_ref, None)):
            p = proj(base + hd * RET_QK_DIM, RET_QK_DIM)
            x1 = p[:, :half]
            x2 = p[:, half:]
            r1 = x1 * cos - x2 * sin
            r2 = x2 * cos + x1 * sin
            if scale is not None:
                r1 = r1 * scale
                r2 = r2 * scale
            c0 = hd * RET_QK_DIM
            out_ref[:, c0:c0 + half] = r1.astype(BF16)
            out_ref[:, c0 + half:c0 + RET_QK_DIM] = r2.astype(BF16)

    for c0 in range(0, RET_V_W, RET_V_DIM):
        v_ref[:, c0:c0 + RET_V_DIM] = proj(OFF_V + c0, RET_V_DIM).astype(BF16)
    for c0 in range(0, RET_V_W, RET_V_DIM):
        g = proj(OFF_G + c0, RET_V_DIM)
        gs_ref[:, c0:c0 + RET_V_DIM] = (g * _sigmoid(g)).astype(BF16)

    for c0 in range(0, 2 * D_MODEL, 512):
        gt = proj(OFF_GT + c0, 512) + bg_ref[:, c0:c0 + 512]
        gt_ref[:, c0:c0 + 512] = _sigmoid(gt).astype(BF16)

    for c0 in range(0, CONV_CH, 256):
        a = proj(OFF_CA + c0, 256)
        b = proj(OFF_CB + c0, 256)
        cpad_ref[CONV_HALO:CONV_HALO + rows, c0:c0 + 256] = a * _sigmoid(b)

    first = CONV_HALO - (CONV_WIDTH - 1)
    for r0 in range(0, rows, CONV_ROW_CHUNK):
        for l0 in range(0, CONV_CH, LANES):
            acc = None
            for res in range(SUBLANES):
                z = None
                for j in range(CONV_WIDTH):
                    off = first + j
                    if off % SUBLANES != res:
                        continue
                    term = cw_ref[j:j + 1, l0:l0 + LANES] * cpad_ref[r0 + off:r0 + off + CONV_ROW_CHUNK,
                                                                     l0:l0 + LANES]
                    z = term if z is None else z + term
                if z is not None:
                    acc = z if acc is None else acc + z
            y_ref[r0:r0 + CONV_ROW_CHUNK, l0:l0 + LANES] = acc + cb_ref[:, l0:l0 + LANES]

    y = y_ref[...]
    mu = jnp.mean(y, axis=-1, keepdims=True)
    d = y - mu
    var = jnp.mean(d * d, axis=-1, keepdims=True)
    z = d * lax.rsqrt(var + LN_EPS) * lg_ref[...] + lb_ref[...]
    cs_ref[...] = (z * _sigmoid(z)).astype(BF16)

    cpad_ref[0:CONV_HALO, :] = cpad_ref[rows:rows + CONV_HALO, :]


def _inproj(h, layer, p, cos, sin):
    seq = h.shape[0]
    t = INPROJ_ROWS
    lyr = lambda *rest: (layer,) + rest
    return pl.pallas_call(
        _inproj_body,
        out_shape=(
            jax.ShapeDtypeStruct((seq, RET_QK_W), BF16),
            jax.ShapeDtypeStruct((seq, RET_QK_W), BF16),
            jax.ShapeDtypeStruct((seq, RET_V_W), BF16),
            jax.ShapeDtypeStruct((seq, RET_V_W), BF16),
            jax.ShapeDtypeStruct((seq, CONV_CH), BF16),
            jax.ShapeDtypeStruct((seq, 2 * D_MODEL), BF16),
        ),
        grid=(seq // t,),
        in_specs=[
            _rows(t, D_MODEL),
            _resident((None, 1, D_MODEL), lyr(0, 0)),
            _resident((None, D_MODEL, IN_W), lyr(0, 0)),
            _rows(t, RET_QK_DIM // 2),
            _rows(t, RET_QK_DIM // 2),
            _resident((None, 1, 2 * D_MODEL), lyr(0, 0)),
            _resident((None, CONV_WIDTH, CONV_CH), lyr(0, 0)),
            _resident((None, 1, CONV_CH), lyr(0, 0)),
            _resident((None, 1, CONV_CH), lyr(0, 0)),
            _resident((None, 1, CONV_CH), lyr(0, 0)),
        ],
        out_specs=(
            _rows(t, RET_QK_W), _rows(t, RET_QK_W), _rows(t, RET_V_W), _rows(t, RET_V_W),
            _rows(t, CONV_CH), _rows(t, 2 * D_MODEL),
        ),
        scratch_shapes=[
            pltpu.VMEM((CONV_HALO + t, CONV_CH), F32),
            pltpu.VMEM((t, CONV_CH), F32),
        ],
        compiler_params=_params("arbitrary"),
        name="inproj",
    )(h, p["norm_mix_g"], p["w_in"], cos, sin, p["b_gate"], p["conv_dw_w"], p["conv_dw_b"],
      p["conv_ln_g"], p["conv_ln_b"])


def _retention_body(q_ref, k_ref, v_ref, gs_ref, dm_ref, dq_ref, dk_ref, db_ref, gn_ref, wo_ref,
                    ya_ref, state_ref, gated_ref):
    @pl.when(pl.program_id(0) == 0)
    def _():
        state_ref[...] = jnp.zeros(state_ref.shape, F32)

    for hd in range(RET_HEADS):
        qk = slice(hd * RET_QK_DIM, (hd + 1) * RET_QK_DIM)
        vv = slice(hd * RET_V_DIM, (hd + 1) * RET_V_DIM)
        q = q_ref[:, qk]
        k = k_ref[:, qk]
        v = v_ref[:, vv]
        s = _dot_nt(q, k) * dm_ref[hd]
        o = _dot(s.astype(BF16), v)
        st = state_ref[hd]
        cross = _dot(q, st.astype(BF16))
        o = o + cross * jnp.concatenate([dq_ref[hd]] * (RET_V_DIM // LANES), axis=-1)
        kd = k.astype(F32) * jnp.concatenate([dk_ref[hd]] * (RET_QK_DIM // LANES), axis=-1)
        state_ref[hd] = st * db_ref[hd] + _dot(kd.T.astype(BF16), v)
        mu = jnp.mean(o, axis=-1, keepdims=True)
        d = o - mu
        var = jnp.mean(d * d, axis=-1, keepdims=True)
        y = d * lax.rsqrt(var + LN_EPS) * gn_ref[:, vv]
        gated_ref[:, vv] = (gs_ref[:, vv].astype(F32) * y).astype(BF16)

    ya_ref[...] = _dot(gated_ref[...], wo_ref[...])


def _retention(q, k, v, gs, layer, p, consts):
    seq = q.shape[0]
    t = RET_ROWS
    dmask, dq, dk, dblk = consts
    lyr = lambda *rest: (layer,) + rest
    return pl.pallas_call(
        _retention_body,
        out_shape=jax.ShapeDtypeStruct((seq, D_MODEL), F32),
        grid=(seq // t,),
        in_specs=[
            _rows(t, RET_QK_W), _rows(t, RET_QK_W), _rows(t, RET_V_W), _rows(t, RET_V_W),
            _resident(dmask.shape, (0, 0, 0)),
            _resident(dq.shape, (0, 0, 0)),
            _resident(dk.shape, (0, 0, 0)),
            _resident(dblk.shape, (0, 0, 0)),
            _resident((None, 1, RET_V_W), lyr(0, 0)),
            _resident((None, RET_V_W, D_MODEL), lyr(0, 0)),
        ],
        out_specs=_rows(t, D_MODEL),
        scratch_shapes=[
            pltpu.VMEM((RET_HEADS, RET_QK_DIM, RET_V_DIM), F32),
            pltpu.VMEM((t, RET_V_W), BF16),
        ],
        compiler_params=_params("arbitrary"),
        name="retention",
    )(q, k, v, gs, dmask, dq, dk, dblk, p["ret_gn_g"], p["w_ret_out"])


def _mix_body(h_ref, ya_ref, cs_ref, gt_ref, wc_ref, bc_ref, wm_ref, out_ref):
    yb = _dot(cs_ref[...], wc_ref[...]) + bc_ref[...]
    ga = gt_ref[:, :D_MODEL].astype(F32)
    gb = gt_ref[:, D_MODEL:].astype(F32)
    m = (ga * ya_ref[...] + gb * yb).astype(BF16)
    out_ref[...] = h_ref[...] + _dot(m, wm_ref[...])


def _mix(h, ya, cs, gt, layer, p):
    seq = h.shape[0]
    t = MIX_ROWS
    lyr = lambda *rest: (layer,) + rest
    return pl.pallas_call(
        _mix_body,
        out_shape=jax.ShapeDtypeStruct((seq, D_MODEL), F32),
        grid=(seq // t,),
        in_specs=[
            _rows(t, D_MODEL), _rows(t, D_MODEL), _rows(t, CONV_CH), _rows(t, 2 * D_MODEL),
            _resident((None, CONV_CH, D_MODEL), lyr(0, 0)),
            _resident((None, 1, D_MODEL), lyr(0, 0)),
            _resident((None, D_MODEL, D_MODEL), lyr(0, 0)),
        ],
        out_specs=_rows(t, D_MODEL),
        compiler_params=_params("parallel"),
        name="mix",
    )(h, ya, cs, gt, p["w_conv_out"], p["b_conv_out"], p["w_mix_out"])


def _memkv_body(mem_ref, g_ref, w_ref, k_ref, v_ref):
    mn = _rms_norm(mem_ref[...], g_ref[...]).astype(BF16)
    kv = _dot(mn, w_ref[...])
    k_ref[...] = kv[:, :D_MODEL].astype(BF16)
    v_ref[...] = kv[:, D_MODEL:].astype(BF16)


def _memkv(mem, norm_g, w_xkv):
    depth = w_xkv.shape[0]
    mlen = mem.shape[0]
    per_layer = lambda *shape: pl.BlockSpec((None,) + shape, lambda l: (l,) + (0,) * len(shape))
    return pl.pallas_call(
        _memkv_body,
        out_shape=(jax.ShapeDtypeStruct((depth, mlen, D_MODEL), BF16),) * 2,
        grid=(depth,),
        in_specs=[
            _resident((mlen, D_MODEL), (0, 0)),
            per_layer(1, D_MODEL),
            per_layer(D_MODEL, 2 * D_MODEL),
        ],
        out_specs=(per_layer(mlen, D_MODEL),) * 2,
        compiler_params=_params("parallel"),
        name="memkv",
    )(mem, norm_g, w_xkv)


def _xattn_body(h_ref, g_ref, wq_ref, km_ref, vm_ref, wo_ref, out_ref, o_ref):
    x = h_ref[...]
    hn = _rms_norm(x, g_ref[...]).astype(BF16)
    for hd in range(X_HEADS):
        cols = slice(hd * X_HEAD_DIM, (hd + 1) * X_HEAD_DIM)
        q = _dot(hn, wq_ref[:, cols]).astype(BF16)
        s = _dot_nt(q, km_ref[:, cols]) * (X_HEAD_DIM ** -0.5)
        e = jnp.exp(s - jnp.max(s, axis=-1, keepdims=True))
        pr = e / jnp.sum(e, axis=-1, keepdims=True)
        o_ref[:, cols] = _dot(pr.astype(BF16), vm_ref[:, cols]).astype(BF16)
    out_ref[...] = x + _dot(o_ref[...], wo_ref[...])


def _xattn(h, km, vm, layer, p):
    seq = h.shape[0]
    t = XATTN_ROWS
    mlen = km.shape[1]
    lyr = lambda *rest: (layer,) + rest
    return pl.pallas_call(
        _xattn_body,
        out_shape=jax.ShapeDtypeStruct((seq, D_MODEL), F32),
        grid=(seq // t,),
        in_specs=[
            _rows(t, D_MODEL),
            _resident((None, 1, D_MODEL), lyr(0, 0)),
            _resident((None, D_MODEL, D_MODEL), lyr(0, 0)),
            _resident((None, mlen, D_MODEL), lyr(0, 0)),
            _resident((None, mlen, D_MODEL), lyr(0, 0)),
            _resident((None, D_MODEL, D_MODEL), lyr(0, 0)),
        ],
        out_specs=_rows(t, D_MODEL),
        scratch_shapes=[pltpu.VMEM((t, D_MODEL), BF16)],
        compiler_params=_params("parallel"),
        name="xattn",
    )(h, p["norm_xattn_g"], p["w_xq"], km, vm, p["w_xo"])


def _ffn_body(h_ref, g_ref, wu_ref, fw_ref, fb_ref, wd_ref, out_ref, act_ref, prev_ref):
    rows = h_ref.shape[0]

    @pl.when(pl.program_id(0) == 0)
    def _():
        prev_ref[...] = jnp.zeros(prev_ref.shape, F32)

    x = h_ref[...]
    hn = _rms_norm(x, g_ref[...]).astype(BF16)
    row = lax.broadcasted_iota(jnp.int32, (rows, FFN_COLS), 0)
    for c0 in range(0, FFN_DIM, FFN_COLS):
        cols = slice(c0, c0 + FFN_COLS)
        val = _dot(hn, wu_ref[:, cols])
        gate = _dot(hn, wu_ref[:, FFN_DIM + c0:FFN_DIM + c0 + FFN_COLS])
        prev = prev_ref[:, cols]
        p1 = prev[SUBLANES - 1:SUBLANES, :]
        p2 = prev[SUBLANES - 2:SUBLANES - 1, :]
        g1 = jnp.where(row == 0, p1, pltpu.roll(gate, 1, axis=0))
        g2 = jnp.where(row == 0, p2, jnp.where(row == 1, p1, pltpu.roll(gate, 2, axis=0)))
        conv = fw_ref[0:1, cols] * g2 + fw_ref[1:2, cols] * g1 + fw_ref[2:3, cols] * gate + fb_ref[:, cols]
        act_ref[:, cols] = (conv * _sigmoid(conv) * val).astype(BF16)
        prev_ref[:, cols] = gate[rows - SUBLANES:, :]
    out_ref[...] = x + _dot(act_ref[...], wd_ref[...])


def _ffn(h, layer, p):
    seq = h.shape[0]
    t = FFN_ROWS
    lyr = lambda *rest: (layer,) + rest
    return pl.pallas_call(
        _ffn_body,
        out_shape=jax.ShapeDtypeStruct((seq, D_MODEL), F32),
        grid=(seq // t,),
        in_specs=[
            _rows(t, D_MODEL),
            _resident((None, 1, D_MODEL), lyr(0, 0)),
            _resident((None, D_MODEL, 2 * FFN_DIM), lyr(0, 0)),
            _resident((None, FFN_CONV_WIDTH, FFN_DIM), lyr(0, 0)),
            _resident((None, 1, FFN_DIM), lyr(0, 0)),
            _resident((None, FFN_DIM, D_MODEL), lyr(0, 0)),
        ],
        out_specs=_rows(t, D_MODEL),
        scratch_shapes=[
            pltpu.VMEM((t, FFN_DIM), BF16),
            pltpu.VMEM((SUBLANES, FFN_DIM), F32),
        ],
        compiler_params=_params("arbitrary"),
        name="ffn",
    )(h, p["norm_ffn_g"], p["w_up"], p["ffn_dw_w"], p["ffn_dw_b"], p["w_down"])


def _final_norm_body(h_ref, g_ref, out_ref):
    out_ref[...] = _rms_norm(h_ref[...], g_ref[...])


def _final_norm(h, g):
    seq = h.shape[0]
    return pl.pallas_call(
        _final_norm_body,
        out_shape=jax.ShapeDtypeStruct((seq, D_MODEL), F32),
        grid=(seq // NORM_ROWS,),
        in_specs=[_rows(NORM_ROWS, D_MODEL), _resident((1, D_MODEL), (0, 0))],
        out_specs=_rows(NORM_ROWS, D_MODEL),
        compiler_params=_params("parallel"),
        name="final_norm",
    )(h, g)


def _retention_constants(block):
    log_gamma = jnp.log(1.0 - jnp.power(2.0, -5.0 - jnp.arange(RET_HEADS, dtype=F32)))
    idx = jnp.arange(block, dtype=jnp.int32)
    n = idx[:, None]
    m = idx[None, :]
    same = (n // CHUNK) == (m // CHUNK)
    earlier = (m // CHUNK) < (n // CHUNK)
    dist = jnp.where(same, jnp.abs(n - m), n - m).astype(F32)
    dmask = jnp.where((same | earlier)[None], jnp.exp(log_gamma[:, None, None] * dist[None]), 0.0)
    r = jnp.arange(block, dtype=F32)
    dq = jnp.exp(log_gamma[:, None] * (r[None, :] + 1.0))
    dk = jnp.exp(log_gamma[:, None] * (block - 1.0 - r[None, :]))
    dq = jnp.broadcast_to(dq[:, :, None], (RET_HEADS, block, LANES))
    dk = jnp.broadcast_to(dk[:, :, None], (RET_HEADS, block, LANES))
    dblk = jnp.broadcast_to(jnp.exp(log_gamma * block)[:, None, None], (RET_HEADS, 1, RET_V_DIM))
    return dmask.astype(F32), dq, dk, dblk


def kernel(x, mem, positions, norm_mix_g, w_in, b_gate, ret_gn_g, w_ret_out, conv_dw_w, conv_dw_b,
           conv_ln_g, conv_ln_b, w_conv_out, b_conv_out, w_mix_out, norm_xattn_g, norm_mem_g, w_xq,
           w_xkv, w_xo, norm_ffn_g, w_up, ffn_dw_w, ffn_dw_b, w_down, norm_final_g):
    batch, seq, d_model = x.shape
    depth = w_in.shape[0]
    assert batch == 1 and d_model == D_MODEL and w_in.shape[2] == IN_W

    vec = lambda a: a.reshape(a.shape[0], 1, a.shape[1])
    p = {
        "norm_mix_g": vec(norm_mix_g), "w_in": w_in.astype(BF16), "b_gate": vec(b_gate),
        "ret_gn_g": vec(ret_gn_g), "w_ret_out": w_ret_out.astype(BF16),
        "conv_dw_w": conv_dw_w, "conv_dw_b": vec(conv_dw_b),
        "conv_ln_g": vec(conv_ln_g), "conv_ln_b": vec(conv_ln_b),
        "w_conv_out": w_conv_out.astype(BF16), "b_conv_out": vec(b_conv_out),
        "w_mix_out": w_mix_out.astype(BF16),
        "norm_xattn_g": vec(norm_xattn_g), "w_xq": w_xq.astype(BF16), "w_xo": w_xo.astype(BF16),
        "norm_ffn_g": vec(norm_ffn_g), "w_up": w_up.astype(BF16),
        "ffn_dw_w": ffn_dw_w, "ffn_dw_b": vec(ffn_dw_b), "w_down": w_down.astype(BF16),
    }

    inv_freq = 1.0 / (ROPE_THETA ** (jnp.arange(0, RET_QK_DIM, 2, dtype=F32) / RET_QK_DIM))
    cos, sin = _rope_tables(positions.astype(F32).reshape(seq, 1), inv_freq.reshape(1, -1))
    consts = _retention_constants(RET_ROWS)
    km, vm = _memkv(mem[0], vec(norm_mem_g), w_xkv.astype(BF16))

    h = x[0]
    for layer in range(depth):
        q, k, v, gs, cs, gt = _inproj(h, layer, p, cos, sin)
        ya = _retention(q, k, v, gs, layer, p, consts)
        h = _mix(h, ya, cs, gt, layer, p)
        h = _xattn(h, km, vm, layer, p)
        h = _ffn(h, layer, p)
    return _final_norm(h, norm_final_g.reshape(1, d_model))[None]
```

```python
import functools

import jax
import jax.numpy as jnp
from jax import lax
from jax.experimental import pallas as pl
from jax.experimental.pallas import tpu as pltpu

F32 = jnp.float32
BF16 = jnp.bfloat16

D_MODEL = 1024
CHUNK = 64
RET_HEADS = 4
RET_QK_DIM = 256
RET_V_DIM = 512
RET_QK_W = RET_HEADS * RET_QK_DIM
RET_V_W = RET_HEADS * RET_V_DIM
ROPE_THETA = 10000.0
CONV_CH = D_MODEL
CONV_WIDTH = 31
X_HEADS = 4
X_HEAD_DIM = D_MODEL // X_HEADS
FFN_DIM = 2816
FFN_CONV_WIDTH = 3
RMS_EPS = 1e-6
LN_EPS = 1e-5
NEG_LOG2_E = -1.4426950408889634

OFF_Q = 0
OFF_K = OFF_Q + RET_QK_W
OFF_V = OFF_K + RET_QK_W
OFF_G = OFF_V + RET_V_W
OFF_CA = OFF_G + RET_V_W
OFF_CB = OFF_CA + CONV_CH
OFF_GT = OFF_CB + CONV_CH
IN_W = OFF_GT + 2 * D_MODEL

LANES = 128
SUBLANES = 8
VMEM_LIMIT = 56 * 1024 * 1024

ROPE_ROWS = 1024
INPROJ_ROWS = 256
RET_ROWS = 256
MIX_ROWS = 512
XATTN_ROWS = 512
FFN_ROWS = 256
FFN_COLS = 256
NORM_ROWS = 1024
CONV_HALO = 32
CONV_ROW_CHUNK = 64
PROJ_COLS = 256
STAGE_BYTES = 5 * 1024 * 1024


def _resident(block_shape, index):
    return pl.BlockSpec(block_shape, lambda i: index, pipeline_mode=pl.Buffered(1))


def _rows(block_rows, width):
    return pl.BlockSpec((block_rows, width), lambda i: (i, 0))


def _params(semantics):
    return pltpu.CompilerParams(dimension_semantics=(semantics,), vmem_limit_bytes=VMEM_LIMIT)


UNBLOCKED = pl.BlockSpec(memory_space=pl.ANY)


def _weight_scratch(k, n):
    limit = STAGE_BYTES // (2 * n * 4)
    stage_rows = SUBLANES
    while 2 * stage_rows <= limit and k % (2 * stage_rows) == 0:
        stage_rows *= 2
    return [pltpu.VMEM((k, n), BF16), pltpu.VMEM((2, stage_rows, n), F32), pltpu.SemaphoreType.DMA((2,))]


def _stage_weight(w_any, layer, w_vmem, stage, sem):
    rows = stage.shape[1]
    chunks = w_vmem.shape[0] // rows

    def copy(c):
        return pltpu.make_async_copy(w_any.at[layer, pl.ds(c * rows, rows), :], stage.at[c % 2], sem.at[c % 2])

    copy(0).start()
    for c in range(chunks):
        if c + 1 < chunks:
            copy(c + 1).start()
        copy(c).wait()
        w_vmem[c * rows:(c + 1) * rows, :] = stage[c % 2].astype(BF16)


def _rms_norm(x, g):
    ms = jnp.mean(x * x, axis=-1, keepdims=True)
    return x * lax.rsqrt(ms + RMS_EPS) * g


def _sigmoid(x):
    return 1.0 / (1.0 + jnp.exp2(x * NEG_LOG2_E))


def _dot(a, b):
    return jnp.dot(a, b, preferred_element_type=F32)


def _dot_nt(a, b):
    return lax.dot_general(a, b, (((1,), (1,)), ((), ())), preferred_element_type=F32)


def _rope_body(pos_ref, invf_ref, cos_ref, sin_ref):
    ang = pos_ref[...] * invf_ref[...]
    cos_ref[...] = jnp.cos(ang)
    sin_ref[...] = jnp.sin(ang)


def _rope_tables(pos, inv_freq):
    seq = pos.shape[0]
    half = inv_freq.shape[1]
    return pl.pallas_call(
        _rope_body,
        out_shape=(jax.ShapeDtypeStruct((seq, half), F32),) * 2,
        grid=(seq // ROPE_ROWS,),
        in_specs=[_rows(ROPE_ROWS, 1), _resident((1, half), (0, 0))],
        out_specs=(_rows(ROPE_ROWS, half),) * 2,
        compiler_params=_params("parallel"),
        name="rope_tables",
    )(pos, inv_freq)


def _inproj_body(h_ref, ng_ref, w_any, cos_ref, sin_ref, bg_ref, cw_ref, cb_ref, lg_ref, lb_ref,
                 q_ref, k_ref, v_ref, gs_ref, cs_ref, gt_ref, cpad_ref, y_ref, w_ref, w_stage, w_sem,
                 *, layer):
    rows = h_ref.shape[0]
    half = RET_QK_DIM // 2

    @pl.when(pl.program_id(0) == 0)
    def _():
        cpad_ref[0:CONV_HALO, :] = jnp.zeros((CONV_HALO, CONV_CH), F32)
        _stage_weight(w_any, layer, w_ref, w_stage, w_sem)

    u = _rms_norm(h_ref[...], ng_ref[...]).astype(BF16)
    cos = cos_ref[...]
    sin = sin_ref[...]
    q_scale = RET_QK_DIM ** -0.5
    cos_q = cos * q_scale
    sin_q = sin * q_scale

    def proj(col, width):
        return _dot(u, w_ref[:, col:col + width])

    def rotary_task(base, out_ref, hd, cs_, sn_):
        def run():
            p = proj(base + hd * RET_QK_DIM, RET_QK_DIM)
            x1 = p[:, :half]
            x2 = p[:, half:]
            c0 = hd * RET_QK_DIM
            out_ref[:, c0:c0 + half] = (x1 * cs_ - x2 * sn_).astype(BF16)
            out_ref[:, c0 + half:c0 + RET_QK_DIM] = (x2 * cs_ + x1 * sn_).astype(BF16)
        return run

    def v_task(c0):
        def run():
            v_ref[:, c0:c0 + PROJ_COLS] = proj(OFF_V + c0, PROJ_COLS).astype(BF16)
        return run

    def g_task(c0):
        def run():
            g = proj(OFF_G + c0, PROJ_COLS)
            gs_ref[:, c0:c0 + PROJ_COLS] = (g * _sigmoid(g)).astype(BF16)
        return run

    def gate_task(c0):
        def run():
            gt = proj(OFF_GT + c0, PROJ_COLS) + bg_ref[:, c0:c0 + PROJ_COLS]
            gt_ref[:, c0:c0 + PROJ_COLS] = _sigmoid(gt).astype(BF16)
        return run

    tasks = []
    for hd in range(RET_HEADS):
        tasks.append(rotary_task(OFF_Q, q_ref, hd, cos_q, sin_q))
        tasks.append(rotary_task(OFF_K, k_ref, hd, cos, sin))
    tasks += [v_task(c0) for c0 in range(0, RET_V_W, PROJ_COLS)]
    tasks += [g_task(c0) for c0 in range(0, RET_V_W, PROJ_COLS)]
    tasks += [gate_task(c0) for c0 in range(0, 2 * D_MODEL, PROJ_COLS)]

    for c0 in range(0, CONV_CH, PROJ_COLS):
        a = proj(OFF_CA + c0, PROJ_COLS)
        b = proj(OFF_CB + c0, PROJ_COLS)
        cpad_ref[CONV_HALO:CONV_HALO + rows, c0:c0 + PROJ_COLS] = a * _sigmoid(b)

    first = CONV_HALO - (CONV_WIDTH - 1)
    lane_groups = CONV_CH // LANES
    per_group = -(-len(tasks) // lane_groups)
    for grp in range(lane_groups):
        l0 = grp * LANES
        for r0 in range(0, rows, CONV_ROW_CHUNK):
            acc = None
            for res in range(SUBLANES):
                z = None
                for j in range(CONV_WIDTH):
                    off = first + j
                    if off % SUBLANES != res:
                        continue
                    term = cw_ref[j:j + 1, l0:l0 + LANES] * cpad_ref[r0 + off:r0 + off + CONV_ROW_CHUNK,
                                                                     l0:l0 + LANES]
                    z = term if z is None else z + term
                if z is not None:
                    acc = z if acc is None else acc + z
            y_ref[r0:r0 + CONV_ROW_CHUNK, l0:l0 + LANES] = acc + cb_ref[:, l0:l0 + LANES]
        for task in tasks[grp * per_group:(grp + 1) * per_group]:
            task()

    y = y_ref[...]
    mu = jnp.mean(y, axis=-1, keepdims=True)
    d = y - mu
    var = jnp.mean(d * d, axis=-1, keepdims=True)
    z = d * lax.rsqrt(var + LN_EPS) * lg_ref[...] + lb_ref[...]
    cs_ref[...] = (z * _sigmoid(z)).astype(BF16)

    cpad_ref[0:CONV_HALO, :] = cpad_ref[rows:rows + CONV_HALO, :]


def _inproj(h, layer, p, cos, sin):
    seq = h.shape[0]
    t = INPROJ_ROWS
    lyr = lambda *rest: (layer,) + rest
    return pl.pallas_call(
        functools.partial(_inproj_body, layer=layer),
        out_shape=(
            jax.ShapeDtypeStruct((seq, RET_QK_W), BF16),
            jax.ShapeDtypeStruct((seq, RET_QK_W), BF16),
            jax.ShapeDtypeStruct((seq, RET_V_W), BF16),
            jax.ShapeDtypeStruct((seq, RET_V_W), BF16),
            jax.ShapeDtypeStruct((seq, CONV_CH), BF16),
            jax.ShapeDtypeStruct((seq, 2 * D_MODEL), BF16),
        ),
        grid=(seq // t,),
        in_specs=[
            _rows(t, D_MODEL),
            _resident((None, 1, D_MODEL), lyr(0, 0)),
            UNBLOCKED,
            _rows(t, RET_QK_DIM // 2),
            _rows(t, RET_QK_DIM // 2),
            _resident((None, 1, 2 * D_MODEL), lyr(0, 0)),
            _resident((None, CONV_WIDTH, CONV_CH), lyr(0, 0)),
            _resident((None, 1, CONV_CH), lyr(0, 0)),
            _resident((None, 1, CONV_CH), lyr(0, 0)),
            _resident((None, 1, CONV_CH), lyr(0, 0)),
        ],
        out_specs=(
            _rows(t, RET_QK_W), _rows(t, RET_QK_W), _rows(t, RET_V_W), _rows(t, RET_V_W),
            _rows(t, CONV_CH), _rows(t, 2 * D_MODEL),
        ),
        scratch_shapes=[
            pltpu.VMEM((CONV_HALO + t, CONV_CH), F32),
            pltpu.VMEM((t, CONV_CH), F32),
            *_weight_scratch(D_MODEL, IN_W),
        ],
        compiler_params=_params("arbitrary"),
        name="inproj",
    )(h, p["norm_mix_g"], p["w_in"], cos, sin, p["b_gate"], p["conv_dw_w"], p["conv_dw_b"],
      p["conv_ln_g"], p["conv_ln_b"])


def _retention_body(q_ref, k_ref, v_ref, gs_ref, dm_ref, dq_ref, dk_ref, db_ref, gn_ref, wo_any,
                    ya_ref, state_ref, gated_ref, wo_ref, wo_stage, wo_sem, *, layer):
    @pl.when(pl.program_id(0) == 0)
    def _():
        state_ref[...] = jnp.zeros(state_ref.shape, F32)
        _stage_weight(wo_any, layer, wo_ref, wo_stage, wo_sem)

    for hd in range(RET_HEADS):
        qk = slice(hd * RET_QK_DIM, (hd + 1) * RET_QK_DIM)
        vv = slice(hd * RET_V_DIM, (hd + 1) * RET_V_DIM)
        q = q_ref[:, qk]
        k = k_ref[:, qk]
        v = v_ref[:, vv]
        s = _dot_nt(q, k) * dm_ref[hd]
        o = _dot(s.astype(BF16), v)
        st = state_ref[hd]
        cross = _dot(q, st.astype(BF16))
        o = o + cross * jnp.concatenate([dq_ref[hd]] * (RET_V_DIM // LANES), axis=-1)
        kd = k.astype(F32) * jnp.concatenate([dk_ref[hd]] * (RET_QK_DIM // LANES), axis=-1)
        state_ref[hd] = st * db_ref[hd] + _dot(kd.T.astype(BF16), v)
        mu = jnp.mean(o, axis=-1, keepdims=True)
        d = o - mu
        var = jnp.mean(d * d, axis=-1, keepdims=True)
        y = d * lax.rsqrt(var + LN_EPS) * gn_ref[:, vv]
        gated_ref[:, vv] = (gs_ref[:, vv].astype(F32) * y).astype(BF16)

    ya_ref[...] = _dot(gated_ref[...], wo_ref[...])


def _retention(q, k, v, gs, layer, p, consts):
    seq = q.shape[0]
    t = RET_ROWS
    dmask, dq, dk, dblk = consts
    lyr = lambda *rest: (layer,) + rest
    return pl.pallas_call(
        functools.partial(_retention_body, layer=layer),
        out_shape=jax.ShapeDtypeStruct((seq, D_MODEL), F32),
        grid=(seq // t,),
        in_specs=[
            _rows(t, RET_QK_W), _rows(t, RET_QK_W), _rows(t, RET_V_W), _rows(t, RET_V_W),
            _resident(dmask.shape, (0, 0, 0)),
            _resident(dq.shape, (0, 0, 0)),
            _resident(dk.shape, (0, 0, 0)),
            _resident(dblk.shape, (0, 0, 0)),
            _resident((None, 1, RET_V_W), lyr(0, 0)),
            UNBLOCKED,
        ],
        out_specs=_rows(t, D_MODEL),
        scratch_shapes=[
            pltpu.VMEM((RET_HEADS, RET_QK_DIM, RET_V_DIM), F32),
            pltpu.VMEM((t, RET_V_W), BF16),
            *_weight_scratch(RET_V_W, D_MODEL),
        ],
        compiler_params=_params("arbitrary"),
        name="retention",
    )(q, k, v, gs, dmask, dq, dk, dblk, p["ret_gn_g"], p["w_ret_out"])


def _mix_body(h_ref, ya_ref, cs_ref, gt_ref, wc_any, bc_ref, wm_any, out_ref,
              wc_ref, wc_stage, wc_sem, wm_ref, wm_stage, wm_sem, *, layer):
    @pl.when(pl.program_id(0) == 0)
    def _():
        _stage_weight(wc_any, layer, wc_ref, wc_stage, wc_sem)
        _stage_weight(wm_any, layer, wm_ref, wm_stage, wm_sem)

    yb = _dot(cs_ref[...], wc_ref[...]) + bc_ref[...]
    ga = gt_ref[:, :D_MODEL].astype(F32)
    gb = gt_ref[:, D_MODEL:].astype(F32)
    m = (ga * ya_ref[...] + gb * yb).astype(BF16)
    out_ref[...] = h_ref[...] + _dot(m, wm_ref[...])


def _mix(h, ya, cs, gt, layer, p):
    seq = h.shape[0]
    t = MIX_ROWS
    lyr = lambda *rest: (layer,) + rest
    return pl.pallas_call(
        functools.partial(_mix_body, layer=layer),
        out_shape=jax.ShapeDtypeStruct((seq, D_MODEL), F32),
        grid=(seq // t,),
        in_specs=[
            _rows(t, D_MODEL), _rows(t, D_MODEL), _rows(t, CONV_CH), _rows(t, 2 * D_MODEL),
            UNBLOCKED,
            _resident((None, 1, D_MODEL), lyr(0, 0)),
            UNBLOCKED,
        ],
        out_specs=_rows(t, D_MODEL),
        scratch_shapes=[*_weight_scratch(CONV_CH, D_MODEL), *_weight_scratch(D_MODEL, D_MODEL)],
        compiler_params=_params("arbitrary"),
        name="mix",
    )(h, ya, cs, gt, p["w_conv_out"], p["b_conv_out"], p["w_mix_out"])


def _memkv_body(mem_ref, g_ref, w_ref, k_ref, v_ref):
    mn = _rms_norm(mem_ref[...], g_ref[...]).astype(BF16)
    kv = _dot(mn, w_ref[...].astype(BF16))
    k_ref[...] = kv[:, :D_MODEL].astype(BF16)
    v_ref[...] = kv[:, D_MODEL:].astype(BF16)


def _memkv(mem, norm_g, w_xkv):
    depth = w_xkv.shape[0]
    mlen = mem.shape[0]
    per_layer = lambda *shape: pl.BlockSpec((None,) + shape, lambda l: (l,) + (0,) * len(shape))
    return pl.pallas_call(
        _memkv_body,
        out_shape=(jax.ShapeDtypeStruct((depth, mlen, D_MODEL), BF16),) * 2,
        grid=(depth,),
        in_specs=[
            _resident((mlen, D_MODEL), (0, 0)),
            per_layer(1, D_MODEL),
            per_layer(D_MODEL, 2 * D_MODEL),
        ],
        out_specs=(per_layer(mlen, D_MODEL),) * 2,
        compiler_params=_params("parallel"),
        name="memkv",
    )(mem, norm_g, w_xkv)


def _xattn_body(h_ref, g_ref, wq_any, km_ref, vm_ref, wo_any, out_ref, o_ref,
                wq_ref, wq_stage, wq_sem, wo_ref, wo_stage, wo_sem, *, layer):
    @pl.when(pl.program_id(0) == 0)
    def _():
        _stage_weight(wq_any, layer, wq_ref, wq_stage, wq_sem)
        _stage_weight(wo_any, layer, wo_ref, wo_stage, wo_sem)

    x = h_ref[...]
    hn = _rms_norm(x, g_ref[...]).astype(BF16)
    for hd in range(X_HEADS):
        cols = slice(hd * X_HEAD_DIM, (hd + 1) * X_HEAD_DIM)
        q = _dot(hn, wq_ref[:, cols]).astype(BF16)
        s = _dot_nt(q, km_ref[:, cols]) * (X_HEAD_DIM ** -0.5)
        e = jnp.exp(s - jnp.max(s, axis=-1, keepdims=True))
        pr = e / jnp.sum(e, axis=-1, keepdims=True)
        o_ref[:, cols] = _dot(pr.astype(BF16), vm_ref[:, cols]).astype(BF16)
    out_ref[...] = x + _dot(o_ref[...], wo_ref[...])


def _xattn(h, km, vm, layer, p):
    seq = h.shape[0]
    t = XATTN_ROWS
    mlen = km.shape[1]
    lyr = lambda *rest: (layer,) + rest
    return pl.pallas_call(
        functools.partial(_xattn_body, layer=layer),
        out_shape=jax.ShapeDtypeStruct((seq, D_MODEL), F32),
        grid=(seq // t,),
        in_specs=[
            _rows(t, D_MODEL),
            _resident((None, 1, D_MODEL), lyr(0, 0)),
            UNBLOCKED,
            _resident((None, mlen, D_MODEL), lyr(0, 0)),
            _resident((None, mlen, D_MODEL), lyr(0, 0)),
            UNBLOCKED,
        ],
        out_specs=_rows(t, D_MODEL),
        scratch_shapes=[pltpu.VMEM((t, D_MODEL), BF16),
                        *_weight_scratch(D_MODEL, D_MODEL), *_weight_scratch(D_MODEL, D_MODEL)],
        compiler_params=_params("arbitrary"),
        name="xattn",
    )(h, p["norm_xattn_g"], p["w_xq"], km, vm, p["w_xo"])


def _ffn_body(h_ref, g_ref, wu_any, fw_ref, fb_ref, wd_any, out_ref, act_ref, prev_ref,
              wu_ref, wu_stage, wu_sem, wd_ref, wd_stage, wd_sem, *, layer):
    rows = h_ref.shape[0]

    @pl.when(pl.program_id(0) == 0)
    def _():
        prev_ref[...] = jnp.zeros(prev_ref.shape, F32)
        _stage_weight(wu_any, layer, wu_ref, wu_stage, wu_sem)
        _stage_weight(wd_any, layer, wd_ref, wd_stage, wd_sem)

    x = h_ref[...]
    hn = _rms_norm(x, g_ref[...]).astype(BF16)
    row = lax.broadcasted_iota(jnp.int32, (rows, FFN_COLS), 0)
    for c0 in range(0, FFN_DIM, FFN_COLS):
        cols = slice(c0, c0 + FFN_COLS)
        val = _dot(hn, wu_ref[:, cols])
        gate = _dot(hn, wu_ref[:, FFN_DIM + c0:FFN_DIM + c0 + FFN_COLS])
        prev = prev_ref[:, cols]
        p1 = prev[SUBLANES - 1:SUBLANES, :]
        p2 = prev[SUBLANES - 2:SUBLANES - 1, :]
        g1 = jnp.where(row == 0, p1, pltpu.roll(gate, 1, axis=0))
        g2 = jnp.where(row == 0, p2, jnp.where(row == 1, p1, pltpu.roll(gate, 2, axis=0)))
        conv = fw_ref[0:1, cols] * g2 + fw_ref[1:2, cols] * g1 + fw_ref[2:3, cols] * gate + fb_ref[:, cols]
        act_ref[:, cols] = (conv * _sigmoid(conv) * val).astype(BF16)
        prev_ref[:, cols] = gate[rows - SUBLANES:, :]
    out_ref[...] = x + _dot(act_ref[...], wd_ref[...])


def _ffn(h, layer, p):
    seq = h.shape[0]
    t = FFN_ROWS
    lyr = lambda *rest: (layer,) + rest
    return pl.pallas_call(
        functools.partial(_ffn_body, layer=layer),
        out_shape=jax.ShapeDtypeStruct((seq, D_MODEL), F32),
        grid=(seq // t,),
        in_specs=[
            _rows(t, D_MODEL),
            _resident((None, 1, D_MODEL), lyr(0, 0)),
            UNBLOCKED,
            _resident((None, FFN_CONV_WIDTH, FFN_DIM), lyr(0, 0)),
            _resident((None, 1, FFN_DIM), lyr(0, 0)),
            UNBLOCKED,
        ],
        out_specs=_rows(t, D_MODEL),
        scratch_shapes=[
            pltpu.VMEM((t, FFN_DIM), BF16),
            pltpu.VMEM((SUBLANES, FFN_DIM), F32),
            *_weight_scratch(D_MODEL, 2 * FFN_DIM), *_weight_scratch(FFN_DIM, D_MODEL),
        ],
        compiler_params=_params("arbitrary"),
        name="ffn",
    )(h, p["norm_ffn_g"], p["w_up"], p["ffn_dw_w"], p["ffn_dw_b"], p["w_down"])


def _final_norm_body(h_ref, g_ref, out_ref):
    out_ref[...] = _rms_norm(h_ref[...], g_ref[...])


def _final_norm(h, g):
    seq = h.shape[0]
    return pl.pallas_call(
        _final_norm_body,
        out_shape=jax.ShapeDtypeStruct((seq, D_MODEL), F32),
        grid=(seq // NORM_ROWS,),
        in_specs=[_rows(NORM_ROWS, D_MODEL), _resident((1, D_MODEL), (0, 0))],
        out_specs=_rows(NORM_ROWS, D_MODEL),
        compiler_params=_params("parallel"),
        name="final_norm",
    )(h, g)


def _retention_constants(block):
    log_gamma = jnp.log(1.0 - jnp.power(2.0, -5.0 - jnp.arange(RET_HEADS, dtype=F32)))
    idx = jnp.arange(block, dtype=jnp.int32)
    n = idx[:, None]
    m = idx[None, :]
    same = (n // CHUNK) == (m // CHUNK)
    earlier = (m // CHUNK) < (n // CHUNK)
    dist = jnp.where(same, jnp.abs(n - m), n - m).astype(F32)
    dmask = jnp.where((same | earlier)[None], jnp.exp(log_gamma[:, None, None] * dist[None]), 0.0)
    r = jnp.arange(block, dtype=F32)
    dq = jnp.exp(log_gamma[:, None] * (r[None, :] + 1.0))
    dk = jnp.exp(log_gamma[:, None] * (block - 1.0 - r[None, :]))
    dq = jnp.broadcast_to(dq[:, :, None], (RET_HEADS, block, LANES))
    dk = jnp.broadcast_to(dk[:, :, None], (RET_HEADS, block, LANES))
    dblk = jnp.broadcast_to(jnp.exp(log_gamma * block)[:, None, None], (RET_HEADS, 1, RET_V_DIM))
    return dmask.astype(F32), dq, dk, dblk


def kernel(x, mem, positions, norm_mix_g, w_in, b_gate, ret_gn_g, w_ret_out, conv_dw_w, conv_dw_b,
           conv_ln_g, conv_ln_b, w_conv_out, b_conv_out, w_mix_out, norm_xattn_g, norm_mem_g, w_xq,
           w_xkv, w_xo, norm_ffn_g, w_up, ffn_dw_w, ffn_dw_b, w_down, norm_final_g):
    batch, seq, d_model = x.shape
    depth = w_in.shape[0]
    assert batch == 1 and d_model == D_MODEL and w_in.shape[2] == IN_W

    vec = lambda a: a.reshape(a.shape[0], 1, a.shape[1])
    p = {
        "norm_mix_g": vec(norm_mix_g), "w_in": w_in, "b_gate": vec(b_gate),
        "ret_gn_g": vec(ret_gn_g), "w_ret_out": w_ret_out,
        "conv_dw_w": conv_dw_w, "conv_dw_b": vec(conv_dw_b),
        "conv_ln_g": vec(conv_ln_g), "conv_ln_b": vec(conv_ln_b),
        "w_conv_out": w_conv_out, "b_conv_out": vec(b_conv_out),
        "w_mix_out": w_mix_out,
        "norm_xattn_g": vec(norm_xattn_g), "w_xq": w_xq, "w_xo": w_xo,
        "norm_ffn_g": vec(norm_ffn_g), "w_up": w_up,
        "ffn_dw_w": ffn_dw_w, "ffn_dw_b": vec(ffn_dw_b), "w_down": w_down,
    }

    inv_freq = 1.0 / (ROPE_THETA ** (jnp.arange(0, RET_QK_DIM, 2, dtype=F32) / RET_QK_DIM))
    cos, sin = _rope_tables(positions.astype(F32).reshape(seq, 1), inv_freq.reshape(1, -1))
    consts = _retention_constants(RET_ROWS)
    km, vm = _memkv(mem[0], vec(norm_mem_g), w_xkv)

    h = x[0]
    for layer in range(depth):
        q, k, v, gs, cs, gt = _inproj(h, layer, p, cos, sin)
        ya = _retention(q, k, v, gs, layer, p, consts)
        h = _mix(h, ya, cs, gt, layer, p)
        h = _xattn(h, km, vm, layer, p)
        h = _ffn(h, layer, p)
    return _final_norm(h, norm_final_g.reshape(1, d_model))[None]
```

```python
import functools

import jax
import jax.numpy as jnp
from jax import lax
from jax.experimental import pallas as pl
from jax.experimental.pallas import tpu as pltpu

F32 = jnp.float32
BF16 = jnp.bfloat16

D_MODEL = 1024
CHUNK = 64
RET_HEADS = 4
RET_QK_DIM = 256
RET_V_DIM = 512
RET_QK_W = RET_HEADS * RET_QK_DIM
RET_V_W = RET_HEADS * RET_V_DIM
ROPE_THETA = 10000.0
CONV_CH = D_MODEL
CONV_WIDTH = 31
X_HEADS = 4
X_HEAD_DIM = D_MODEL // X_HEADS
FFN_DIM = 2816
FFN_CONV_WIDTH = 3
RMS_EPS = 1e-6
LN_EPS = 1e-5
NEG_LOG2_E = -1.4426950408889634

OFF_Q = 0
OFF_K = OFF_Q + RET_QK_W
OFF_V = OFF_K + RET_QK_W
OFF_G = OFF_V + RET_V_W
OFF_CA = OFF_G + RET_V_W
OFF_CB = OFF_CA + CONV_CH
OFF_GT = OFF_CB + CONV_CH
IN_W = OFF_GT + 2 * D_MODEL

LANES = 128
SUBLANES = 8
VMEM_LIMIT = 56 * 1024 * 1024

ROPE_ROWS = 1024
INPROJ_ROWS = 256
RET_ROWS = 256
XATTN_ROWS = 512
FFN_ROWS = 256
FFN_COLS = 256
CONV_HALO = 32
CONV_ROW_CHUNK = 64
PROJ_COLS = 256
STAGE_BYTES = 6 * 1024 * 1024
STAGE_SLOTS = 4


def _resident(block_shape, index):
    return pl.BlockSpec(block_shape, lambda i: index, pipeline_mode=pl.Buffered(1))


def _rows(block_rows, width):
    return pl.BlockSpec((block_rows, width), lambda i: (i, 0))


def _params(semantics):
    return pltpu.CompilerParams(dimension_semantics=(semantics,), vmem_limit_bytes=VMEM_LIMIT)


UNBLOCKED = pl.BlockSpec(memory_space=pl.ANY)


def _stage_rows(n, ks):
    limit = STAGE_BYTES // (STAGE_SLOTS * n * 4)
    rows = SUBLANES
    while 2 * rows <= limit and all(k % (2 * rows) == 0 for k in ks):
        rows *= 2
    return rows


def _staging(n, ks):
    return [pltpu.VMEM((STAGE_SLOTS, _stage_rows(n, ks), n), F32), pltpu.SemaphoreType.DMA((STAGE_SLOTS,))]


def _stage_weight(w_any, layer, w_vmem, stage, sem):
    slots, rows, _ = stage.shape
    chunks = w_vmem.shape[0] // rows

    def copy(c):
        return pltpu.make_async_copy(w_any.at[layer, pl.ds(c * rows, rows), :], stage.at[c % slots],
                                     sem.at[c % slots])

    for c in range(min(slots - 1, chunks)):
        copy(c).start()
    for c in range(chunks):
        if c + slots - 1 < chunks:
            copy(c + slots - 1).start()
        copy(c).wait()
        w_vmem[c * rows:(c + 1) * rows, :] = stage[c % slots].astype(BF16)


def _rms_norm(x, g):
    ms = jnp.mean(x * x, axis=-1, keepdims=True)
    return x * lax.rsqrt(ms + RMS_EPS) * g


def _sigmoid(x):
    return 1.0 / (1.0 + jnp.exp2(x * NEG_LOG2_E))


def _dot(a, b):
    return jnp.dot(a, b, preferred_element_type=F32)


def _dot_nt(a, b):
    return lax.dot_general(a, b, (((1,), (1,)), ((), ())), preferred_element_type=F32)


def _rope_body(pos_ref, invf_ref, cos_ref, sin_ref):
    ang = pos_ref[...] * invf_ref[...]
    cos_ref[...] = jnp.cos(ang)
    sin_ref[...] = jnp.sin(ang)


def _rope_tables(pos, inv_freq):
    seq = pos.shape[0]
    half = inv_freq.shape[1]
    return pl.pallas_call(
        _rope_body,
        out_shape=(jax.ShapeDtypeStruct((seq, half), F32),) * 2,
        grid=(seq // ROPE_ROWS,),
        in_specs=[_rows(ROPE_ROWS, 1), _resident((1, half), (0, 0))],
        out_specs=(_rows(ROPE_ROWS, half),) * 2,
        compiler_params=_params("parallel"),
        name="rope_tables",
    )(pos, inv_freq)


def _inproj_body(h_ref, ng_ref, w_any, cos_ref, sin_ref, bg_ref, cw_ref, cb_ref, lg_ref, lb_ref,
                 q_ref, k_ref, v_ref, gs_ref, cs_ref, gt_ref, cpad_ref, y_ref, w_ref, w_stage, w_sem,
                 *, layer):
    rows = h_ref.shape[0]
    half = RET_QK_DIM // 2

    @pl.when(pl.program_id(0) == 0)
    def _():
        cpad_ref[0:CONV_HALO, :] = jnp.zeros((CONV_HALO, CONV_CH), F32)
        _stage_weight(w_any, layer, w_ref, w_stage, w_sem)

    u = _rms_norm(h_ref[...], ng_ref[...]).astype(BF16)
    cos = cos_ref[...]
    sin = sin_ref[...]
    q_scale = RET_QK_DIM ** -0.5
    cos_q = cos * q_scale
    sin_q = sin * q_scale

    def proj(col, width):
        return _dot(u, w_ref[:, col:col + width])

    def rotary_task(base, out_ref, hd, cs_, sn_):
        def run():
            p = proj(base + hd * RET_QK_DIM, RET_QK_DIM)
            x1 = p[:, :half]
            x2 = p[:, half:]
            c0 = hd * RET_QK_DIM
            out_ref[:, c0:c0 + half] = (x1 * cs_ - x2 * sn_).astype(BF16)
            out_ref[:, c0 + half:c0 + RET_QK_DIM] = (x2 * cs_ + x1 * sn_).astype(BF16)
        return run

    def v_task(c0):
        def run():
            v_ref[:, c0:c0 + PROJ_COLS] = proj(OFF_V + c0, PROJ_COLS).astype(BF16)
        return run

    def g_task(c0):
        def run():
            g = proj(OFF_G + c0, PROJ_COLS)
            gs_ref[:, c0:c0 + PROJ_COLS] = (g * _sigmoid(g)).astype(BF16)
        return run

    def gate_task(c0):
        def run():
            gt = proj(OFF_GT + c0, PROJ_COLS) + bg_ref[:, c0:c0 + PROJ_COLS]
            gt_ref[:, c0:c0 + PROJ_COLS] = _sigmoid(gt).astype(BF16)
        return run

    tasks = []
    for hd in range(RET_HEADS):
        tasks.append(rotary_task(OFF_Q, q_ref, hd, cos_q, sin_q))
        tasks.append(rotary_task(OFF_K, k_ref, hd, cos, sin))
    tasks += [v_task(c0) for c0 in range(0, RET_V_W, PROJ_COLS)]
    tasks += [g_task(c0) for c0 in range(0, RET_V_W, PROJ_COLS)]
    tasks += [gate_task(c0) for c0 in range(0, 2 * D_MODEL, PROJ_COLS)]

    for c0 in range(0, CONV_CH, PROJ_COLS):
        a = proj(OFF_CA + c0, PROJ_COLS)
        b = proj(OFF_CB + c0, PROJ_COLS)
        cpad_ref[CONV_HALO:CONV_HALO + rows, c0:c0 + PROJ_COLS] = a * _sigmoid(b)

    first = CONV_HALO - (CONV_WIDTH - 1)
    lane_groups = CONV_CH // LANES
    per_group = -(-len(tasks) // lane_groups)
    for grp in range(lane_groups):
        l0 = grp * LANES
        for r0 in range(0, rows, CONV_ROW_CHUNK):
            acc = None
            for res in range(SUBLANES):
                z = None
                for j in range(CONV_WIDTH):
                    off = first + j
                    if off % SUBLANES != res:
                        continue
                    term = cw_ref[j:j + 1, l0:l0 + LANES] * cpad_ref[r0 + off:r0 + off + CONV_ROW_CHUNK,
                                                                     l0:l0 + LANES]
                    z = term if z is None else z + term
                if z is not None:
                    acc = z if acc is None else acc + z
            y_ref[r0:r0 + CONV_ROW_CHUNK, l0:l0 + LANES] = acc + cb_ref[:, l0:l0 + LANES]
        for task in tasks[grp * per_group:(grp + 1) * per_group]:
            task()

    y = y_ref[...]
    mu = jnp.mean(y, axis=-1, keepdims=True)
    d = y - mu
    var = jnp.mean(d * d, axis=-1, keepdims=True)
    z = d * lax.rsqrt(var + LN_EPS) * lg_ref[...] + lb_ref[...]
    cs_ref[...] = (z * _sigmoid(z)).astype(BF16)

    cpad_ref[0:CONV_HALO, :] = cpad_ref[rows:rows + CONV_HALO, :]


def _inproj(h, layer, p, cos, sin):
    seq = h.shape[0]
    t = INPROJ_ROWS
    lyr = lambda *rest: (layer,) + rest
    return pl.pallas_call(
        functools.partial(_inproj_body, layer=layer),
        out_shape=(
            jax.ShapeDtypeStruct((seq, RET_QK_W), BF16),
            jax.ShapeDtypeStruct((seq, RET_QK_W), BF16),
            jax.ShapeDtypeStruct((seq, RET_V_W), BF16),
            jax.ShapeDtypeStruct((seq, RET_V_W), BF16),
            jax.ShapeDtypeStruct((seq, CONV_CH), BF16),
            jax.ShapeDtypeStruct((seq, 2 * D_MODEL), BF16),
        ),
        grid=(seq // t,),
        in_specs=[
            _rows(t, D_MODEL),
            _resident((None, 1, D_MODEL), lyr(0, 0)),
            UNBLOCKED,
            _rows(t, RET_QK_DIM // 2),
            _rows(t, RET_QK_DIM // 2),
            _resident((None, 1, 2 * D_MODEL), lyr(0, 0)),
            _resident((None, CONV_WIDTH, CONV_CH), lyr(0, 0)),
            _resident((None, 1, CONV_CH), lyr(0, 0)),
            _resident((None, 1, CONV_CH), lyr(0, 0)),
            _resident((None, 1, CONV_CH), lyr(0, 0)),
        ],
        out_specs=(
            _rows(t, RET_QK_W), _rows(t, RET_QK_W), _rows(t, RET_V_W), _rows(t, RET_V_W),
            _rows(t, CONV_CH), _rows(t, 2 * D_MODEL),
        ),
        scratch_shapes=[
            pltpu.VMEM((CONV_HALO + t, CONV_CH), F32),
            pltpu.VMEM((t, CONV_CH), F32),
            pltpu.VMEM((D_MODEL, IN_W), BF16),
            *_staging(IN_W, [D_MODEL]),
        ],
        compiler_params=_params("arbitrary"),
        name="inproj",
    )(h, p["norm_mix_g"], p["w_in"], cos, sin, p["b_gate"], p["conv_dw_w"], p["conv_dw_b"],
      p["conv_ln_g"], p["conv_ln_b"])


def _retention_body(q_ref, k_ref, v_ref, gs_ref, h_ref, cs_ref, gt_ref, dm_ref, dq_ref, dk_ref, db_ref,
                    gn_ref, bc_ref, wo_any, wc_any, wm_any, out_ref,
                    state_ref, gated_ref, wo_ref, wc_ref, wm_ref, w_stage, w_sem, *, layer):
    @pl.when(pl.program_id(0) == 0)
    def _():
        state_ref[...] = jnp.zeros(state_ref.shape, F32)
        _stage_weight(wo_any, layer, wo_ref, w_stage, w_sem)
        _stage_weight(wc_any, layer, wc_ref, w_stage, w_sem)
        _stage_weight(wm_any, layer, wm_ref, w_stage, w_sem)

    for hd in range(RET_HEADS):
        qk = slice(hd * RET_QK_DIM, (hd + 1) * RET_QK_DIM)
        vv = slice(hd * RET_V_DIM, (hd + 1) * RET_V_DIM)
        q = q_ref[:, qk]
        k = k_ref[:, qk]
        v = v_ref[:, vv]
        s = _dot_nt(q, k) * dm_ref[hd]
        o = _dot(s.astype(BF16), v)
        st = state_ref[hd]
        cross = _dot(q, st.astype(BF16))
        o = o + cross * jnp.concatenate([dq_ref[hd]] * (RET_V_DIM // LANES), axis=-1)
        kd = k.astype(F32) * jnp.concatenate([dk_ref[hd]] * (RET_QK_DIM // LANES), axis=-1)
        state_ref[hd] = st * db_ref[hd] + _dot(kd.T.astype(BF16), v)
        mu = jnp.mean(o, axis=-1, keepdims=True)
        d = o - mu
        var = jnp.mean(d * d, axis=-1, keepdims=True)
        y = d * lax.rsqrt(var + LN_EPS) * gn_ref[:, vv]
        gated_ref[:, vv] = (gs_ref[:, vv].astype(F32) * y).astype(BF16)

    ya = _dot(gated_ref[...], wo_ref[...])
    yb = _dot(cs_ref[...], wc_ref[...]) + bc_ref[...]
    ga = gt_ref[:, :D_MODEL].astype(F32)
    gb = gt_ref[:, D_MODEL:].astype(F32)
    m = (ga * ya + gb * yb).astype(BF16)
    out_ref[...] = h_ref[...] + _dot(m, wm_ref[...])


def _retention(h, q, k, v, gs, cs, gt, layer, p, consts):
    seq = q.shape[0]
    t = RET_ROWS
    dmask, dq, dk, dblk = consts
    lyr = lambda *rest: (layer,) + rest
    return pl.pallas_call(
        functools.partial(_retention_body, layer=layer),
        out_shape=jax.ShapeDtypeStruct((seq, D_MODEL), F32),
        grid=(seq // t,),
        in_specs=[
            _rows(t, RET_QK_W), _rows(t, RET_QK_W), _rows(t, RET_V_W), _rows(t, RET_V_W),
            _rows(t, D_MODEL), _rows(t, CONV_CH), _rows(t, 2 * D_MODEL),
            _resident(dmask.shape, (0, 0, 0)),
            _resident(dq.shape, (0, 0, 0)),
            _resident(dk.shape, (0, 0, 0)),
            _resident(dblk.shape, (0, 0, 0)),
            _resident((None, 1, RET_V_W), lyr(0, 0)),
            _resident((None, 1, D_MODEL), lyr(0, 0)),
            UNBLOCKED, UNBLOCKED, UNBLOCKED,
        ],
        out_specs=_rows(t, D_MODEL),
        scratch_shapes=[
            pltpu.VMEM((RET_HEADS, RET_QK_DIM, RET_V_DIM), F32),
            pltpu.VMEM((t, RET_V_W), BF16),
            pltpu.VMEM((RET_V_W, D_MODEL), BF16),
            pltpu.VMEM((CONV_CH, D_MODEL), BF16),
            pltpu.VMEM((D_MODEL, D_MODEL), BF16),
            *_staging(D_MODEL, [RET_V_W, CONV_CH, D_MODEL]),
        ],
        compiler_params=_params("arbitrary"),
        name="retention_mix",
    )(q, k, v, gs, h, cs, gt, dmask, dq, dk, dblk, p["ret_gn_g"], p["b_conv_out"],
      p["w_ret_out"], p["w_conv_out"], p["w_mix_out"])


def _xprep_body(mem_ref, g_ref, wkv_ref, wq_ref, wo_ref, a_ref, b_ref):
    mn = _rms_norm(mem_ref[...], g_ref[...]).astype(BF16)
    kv = _dot(mn, wkv_ref[...].astype(BF16))
    for hd in range(X_HEADS):
        cols = slice(hd * X_HEAD_DIM, (hd + 1) * X_HEAD_DIM)
        k_h = kv[:, cols].astype(BF16)
        v_h = kv[:, D_MODEL + hd * X_HEAD_DIM:D_MODEL + (hd + 1) * X_HEAD_DIM].astype(BF16)
        a_ref[:, cols] = (_dot_nt(wq_ref[:, cols].astype(BF16), k_h) * (X_HEAD_DIM ** -0.5)).astype(BF16)
        b_ref[cols, :] = _dot(v_h, wo_ref[cols, :].astype(BF16)).astype(BF16)


def _xprep(mem, norm_g, w_xkv, w_xq, w_xo):
    depth = w_xkv.shape[0]
    mlen = mem.shape[0]
    assert mlen == X_HEAD_DIM
    per_layer = lambda *shape: pl.BlockSpec((None,) + shape, lambda l: (l,) + (0,) * len(shape))
    return pl.pallas_call(
        _xprep_body,
        out_shape=(jax.ShapeDtypeStruct((depth, D_MODEL, X_HEADS * mlen), BF16),
                   jax.ShapeDtypeStruct((depth, X_HEADS * mlen, D_MODEL), BF16)),
        grid=(depth,),
        in_specs=[
            _resident((mlen, D_MODEL), (0, 0)),
            per_layer(1, D_MODEL),
            per_layer(D_MODEL, 2 * D_MODEL),
            per_layer(D_MODEL, D_MODEL),
            per_layer(D_MODEL, D_MODEL),
        ],
        out_specs=(per_layer(D_MODEL, X_HEADS * mlen), per_layer(X_HEADS * mlen, D_MODEL)),
        compiler_params=_params("parallel"),
        name="xprep",
    )(mem, norm_g, w_xkv, w_xq, w_xo)


def _xattn_body(h_ref, g_ref, a_ref, b_ref, out_ref, p_ref):
    x = h_ref[...]
    hn = _rms_norm(x, g_ref[...]).astype(BF16)
    s = _dot(hn, a_ref[...])
    mlen = s.shape[1] // X_HEADS
    for hd in range(X_HEADS):
        cols = slice(hd * mlen, (hd + 1) * mlen)
        sh = s[:, cols]
        e = jnp.exp(sh - jnp.max(sh, axis=-1, keepdims=True))
        p_ref[:, cols] = (e * (1.0 / jnp.sum(e, axis=-1, keepdims=True))).astype(BF16)
    out_ref[...] = x + _dot(p_ref[...], b_ref[...])


def _xattn(h, xa, xb, layer, p):
    seq = h.shape[0]
    t = XATTN_ROWS
    lyr = lambda *rest: (layer,) + rest
    return pl.pallas_call(
        _xattn_body,
        out_shape=jax.ShapeDtypeStruct((seq, D_MODEL), F32),
        grid=(seq // t,),
        in_specs=[
            _rows(t, D_MODEL),
            _resident((None, 1, D_MODEL), lyr(0, 0)),
            _resident((None,) + xa.shape[1:], lyr(0, 0)),
            _resident((None,) + xb.shape[1:], lyr(0, 0)),
        ],
        out_specs=_rows(t, D_MODEL),
        scratch_shapes=[pltpu.VMEM((t, xa.shape[2]), BF16)],
        compiler_params=_params("parallel"),
        name="xattn",
    )(h, p["norm_xattn_g"], xa, xb)


def _ffn_body(h_ref, g_ref, wu_any, fw_ref, fb_ref, wd_any, fg_ref, out_ref, act_ref, prev_ref,
              wu_ref, wu_stage, wu_sem, wd_ref, wd_stage, wd_sem, *, layer, final_norm):
    rows = h_ref.shape[0]

    @pl.when(pl.program_id(0) == 0)
    def _():
        prev_ref[...] = jnp.zeros(prev_ref.shape, F32)
        _stage_weight(wu_any, layer, wu_ref, wu_stage, wu_sem)
        _stage_weight(wd_any, layer, wd_ref, wd_stage, wd_sem)

    x = h_ref[...]
    hn = _rms_norm(x, g_ref[...]).astype(BF16)
    row = lax.broadcasted_iota(jnp.int32, (rows, FFN_COLS), 0)
    for c0 in range(0, FFN_DIM, FFN_COLS):
        cols = slice(c0, c0 + FFN_COLS)
        val = _dot(hn, wu_ref[:, cols])
        gate = _dot(hn, wu_ref[:, FFN_DIM + c0:FFN_DIM + c0 + FFN_COLS])
        prev = prev_ref[:, cols]
        p1 = prev[SUBLANES - 1:SUBLANES, :]
        p2 = prev[SUBLANES - 2:SUBLANES - 1, :]
        g1 = jnp.where(row == 0, p1, pltpu.roll(gate, 1, axis=0))
        g2 = jnp.where(row == 0, p2, jnp.where(row == 1, p1, pltpu.roll(gate, 2, axis=0)))
        conv = fw_ref[0:1, cols] * g2 + fw_ref[1:2, cols] * g1 + fw_ref[2:3, cols] * gate + fb_ref[:, cols]
        act_ref[:, cols] = (conv * _sigmoid(conv) * val).astype(BF16)
        prev_ref[:, cols] = gate[rows - SUBLANES:, :]
    out = x + _dot(act_ref[...], wd_ref[...])
    if final_norm:
        out = _rms_norm(out, fg_ref[...])
    out_ref[...] = out


def _ffn(h, layer, p, final_g, final_norm):
    seq = h.shape[0]
    t = FFN_ROWS
    lyr = lambda *rest: (layer,) + rest
    return pl.pallas_call(
        functools.partial(_ffn_body, layer=layer, final_norm=final_norm),
        out_shape=jax.ShapeDtypeStruct((seq, D_MODEL), F32),
        grid=(seq // t,),
        in_specs=[
            _rows(t, D_MODEL),
            _resident((None, 1, D_MODEL), lyr(0, 0)),
            UNBLOCKED,
            _resident((None, FFN_CONV_WIDTH, FFN_DIM), lyr(0, 0)),
            _resident((None, 1, FFN_DIM), lyr(0, 0)),
            UNBLOCKED,
            _resident((1, D_MODEL), (0, 0)),
        ],
        out_specs=_rows(t, D_MODEL),
        scratch_shapes=[
            pltpu.VMEM((t, FFN_DIM), BF16),
            pltpu.VMEM((SUBLANES, FFN_DIM), F32),
            pltpu.VMEM((D_MODEL, 2 * FFN_DIM), BF16), *_staging(2 * FFN_DIM, [D_MODEL]),
            pltpu.VMEM((FFN_DIM, D_MODEL), BF16), *_staging(D_MODEL, [FFN_DIM]),
        ],
        compiler_params=_params("arbitrary"),
        name="ffn",
    )(h, p["norm_ffn_g"], p["w_up"], p["ffn_dw_w"], p["ffn_dw_b"], p["w_down"], final_g)


def _retention_constants(block):
    log_gamma = jnp.log(1.0 - jnp.power(2.0, -5.0 - jnp.arange(RET_HEADS, dtype=F32)))
    idx = jnp.arange(block, dtype=jnp.int32)
    n = idx[:, None]
    m = idx[None, :]
    same = (n // CHUNK) == (m // CHUNK)
    earlier = (m // CHUNK) < (n // CHUNK)
    dist = jnp.where(same, jnp.abs(n - m), n - m).astype(F32)
    dmask = jnp.where((same | earlier)[None], jnp.exp(log_gamma[:, None, None] * dist[None]), 0.0)
    r = jnp.arange(block, dtype=F32)
    dq = jnp.exp(log_gamma[:, None] * (r[None, :] + 1.0))
    dk = jnp.exp(log_gamma[:, None] * (block - 1.0 - r[None, :]))
    dq = jnp.broadcast_to(dq[:, :, None], (RET_HEADS, block, LANES))
    dk = jnp.broadcast_to(dk[:, :, None], (RET_HEADS, block, LANES))
    dblk = jnp.broadcast_to(jnp.exp(log_gamma * block)[:, None, None], (RET_HEADS, 1, RET_V_DIM))
    return dmask.astype(F32), dq, dk, dblk


def kernel(x, mem, positions, norm_mix_g, w_in, b_gate, ret_gn_g, w_ret_out, conv_dw_w, conv_dw_b,
           conv_ln_g, conv_ln_b, w_conv_out, b_conv_out, w_mix_out, norm_xattn_g, norm_mem_g, w_xq,
           w_xkv, w_xo, norm_ffn_g, w_up, ffn_dw_w, ffn_dw_b, w_down, norm_final_g):
    batch, seq, d_model = x.shape
    depth = w_in.shape[0]
    assert batch == 1 and d_model == D_MODEL and w_in.shape[2] == IN_W

    vec = lambda a: a.reshape(a.shape[0], 1, a.shape[1])
    p = {
        "norm_mix_g": vec(norm_mix_g), "w_in": w_in, "b_gate": vec(b_gate),
        "ret_gn_g": vec(ret_gn_g), "w_ret_out": w_ret_out,
        "conv_dw_w": conv_dw_w, "conv_dw_b": vec(conv_dw_b),
        "conv_ln_g": vec(conv_ln_g), "conv_ln_b": vec(conv_ln_b),
        "w_conv_out": w_conv_out, "b_conv_out": vec(b_conv_out),
        "w_mix_out": w_mix_out,
        "norm_xattn_g": vec(norm_xattn_g),
        "norm_ffn_g": vec(norm_ffn_g), "w_up": w_up,
        "ffn_dw_w": ffn_dw_w, "ffn_dw_b": vec(ffn_dw_b), "w_down": w_down,
    }

    inv_freq = 1.0 / (ROPE_THETA ** (jnp.arange(0, RET_QK_DIM, 2, dtype=F32) / RET_QK_DIM))
    cos, sin = _rope_tables(positions.astype(F32).reshape(seq, 1), inv_freq.reshape(1, -1))
    consts = _retention_constants(RET_ROWS)
    xa, xb = _xprep(mem[0], vec(norm_mem_g), w_xkv, w_xq, w_xo)
    final_g = norm_final_g.reshape(1, d_model)

    h = x[0]
    for layer in range(depth):
        q, k, v, gs, cs, gt = _inproj(h, layer, p, cos, sin)
        h = _retention(h, q, k, v, gs, cs, gt, layer, p, consts)
        h = _xattn(h, xa, xb, layer, p)
        h = _ffn(h, layer, p, final_g, final_norm=(layer == depth - 1))
    return h[None]
```

```python
import functools

import jax
import jax.numpy as jnp
from jax import lax
from jax.experimental import pallas as pl
from jax.experimental.pallas import tpu as pltpu

F32 = jnp.float32
BF16 = jnp.bfloat16

D_MODEL = 1024
CHUNK = 64
RET_HEADS = 4
RET_QK_DIM = 256
RET_V_DIM = 512
RET_QK_W = RET_HEADS * RET_QK_DIM
RET_V_W = RET_HEADS * RET_V_DIM
ROPE_THETA = 10000.0
CONV_CH = D_MODEL
CONV_WIDTH = 31
X_HEADS = 4
X_HEAD_DIM = D_MODEL // X_HEADS
FFN_DIM = 2816
FFN_CONV_WIDTH = 3
RMS_EPS = 1e-6
LN_EPS = 1e-5
NEG_LOG2_E = -1.4426950408889634

OFF_Q = 0
OFF_K = OFF_Q + RET_QK_W
OFF_V = OFF_K + RET_QK_W
OFF_G = OFF_V + RET_V_W
OFF_CA = OFF_G + RET_V_W
OFF_CB = OFF_CA + CONV_CH
OFF_GT = OFF_CB + CONV_CH
IN_W = OFF_GT + 2 * D_MODEL

LANES = 128
SUBLANES = 8
BF16_ROWS = 16
VMEM_LIMIT = 56 * 1024 * 1024

ROPE_ROWS = 1024
INPROJ_ROWS = 256
RET_BLOCK = 256
RET_ROWS = 512
XATTN_ROWS = 1024
FFN_ROWS = 512
FFN_COLS = 256
CONV_HALO = 32
CONV_ROW_CHUNK = 64
PROJ_COLS = 256
STAGE_SLOT_BYTES = 11 * 512 * 1024
STAGE_SLOTS = 2


def _resident(block_shape, index):
    return pl.BlockSpec(block_shape, lambda i: index, pipeline_mode=pl.Buffered(1))


def _rows(block_rows, width):
    return pl.BlockSpec((block_rows, width), lambda i: (i, 0))


def _params(semantics):
    return pltpu.CompilerParams(dimension_semantics=(semantics,), vmem_limit_bytes=VMEM_LIMIT)


UNBLOCKED = pl.BlockSpec(memory_space=pl.ANY)


def _stage_rows(n, ks):
    fits = [r for r in range(BF16_ROWS, min(ks) + 1, BF16_ROWS)
            if all(k % r == 0 for k in ks) and r * n * 4 <= STAGE_SLOT_BYTES]
    return max(fits)


def _staging(n, ks):
    return [pltpu.VMEM((STAGE_SLOTS, _stage_rows(n, ks), n), F32), pltpu.SemaphoreType.DMA((STAGE_SLOTS,))]


def _stage_weight(w_any, layer, w_vmem, stage, sem):
    slots, rows, _ = stage.shape
    chunks = w_vmem.shape[0] // rows

    def copy(c):
        return pltpu.make_async_copy(w_any.at[layer, pl.ds(c * rows, rows), :], stage.at[c % slots],
                                     sem.at[c % slots])

    for c in range(min(slots, chunks)):
        copy(c).start(priority=c % slots)
    for c in range(chunks):
        copy(c).wait()
        w_vmem[c * rows:(c + 1) * rows, :] = stage[c % slots].astype(BF16)
        if c + slots < chunks:
            copy(c + slots).start(priority=c % slots)


def _rms_norm(x, g):
    ms = jnp.mean(x * x, axis=-1, keepdims=True)
    return x * lax.rsqrt(ms + RMS_EPS) * g


def _sigmoid(x):
    return 1.0 / (1.0 + jnp.exp2(x * NEG_LOG2_E))


def _dot(a, b):
    return jnp.dot(a, b, preferred_element_type=F32)


def _dot_nt(a, b):
    return lax.dot_general(a, b, (((1,), (1,)), ((), ())), preferred_element_type=F32)


def _rope_body(pos_ref, invf_ref, cos_ref, sin_ref):
    ang = pos_ref[...] * invf_ref[...]
    cos_ref[...] = jnp.cos(ang)
    sin_ref[...] = jnp.sin(ang)


def _rope_tables(pos, inv_freq):
    seq = pos.shape[0]
    half = inv_freq.shape[1]
    return pl.pallas_call(
        _rope_body,
        out_shape=(jax.ShapeDtypeStruct((seq, half), F32),) * 2,
        grid=(seq // ROPE_ROWS,),
        in_specs=[_rows(ROPE_ROWS, 1), _resident((1, half), (0, 0))],
        out_specs=(_rows(ROPE_ROWS, half),) * 2,
        compiler_params=_params("parallel"),
        name="rope_tables",
    )(pos, inv_freq)


def _inproj_body(h_ref, ng_ref, w_any, cos_ref, sin_ref, bg_ref, cw_ref, cb_ref, lg_ref, lb_ref,
                 q_ref, k_ref, v_ref, gs_ref, cs_ref, gt_ref, cpad_ref, y_ref, w_ref, w_stage, w_sem,
                 *, layer):
    rows = h_ref.shape[0]
    half = RET_QK_DIM // 2

    @pl.when(pl.program_id(0) == 0)
    def _():
        cpad_ref[0:CONV_HALO, :] = jnp.zeros((CONV_HALO, CONV_CH), F32)
        _stage_weight(w_any, layer, w_ref, w_stage, w_sem)

    u = _rms_norm(h_ref[...], ng_ref[...]).astype(BF16)
    cos = cos_ref[...]
    sin = sin_ref[...]
    q_scale = RET_QK_DIM ** -0.5
    cos_q = cos * q_scale
    sin_q = sin * q_scale

    def proj(col, width):
        return _dot(u, w_ref[:, col:col + width])

    def rotary_task(base, out_ref, hd, cs_, sn_):
        def run():
            p = proj(base + hd * RET_QK_DIM, RET_QK_DIM)
            x1 = p[:, :half]
            x2 = p[:, half:]
            c0 = hd * RET_QK_DIM
            out_ref[:, c0:c0 + half] = (x1 * cs_ - x2 * sn_).astype(BF16)
            out_ref[:, c0 + half:c0 + RET_QK_DIM] = (x2 * cs_ + x1 * sn_).astype(BF16)
        return run

    def v_task(c0):
        def run():
            v_ref[:, c0:c0 + PROJ_COLS] = proj(OFF_V + c0, PROJ_COLS).astype(BF16)
        return run

    def g_task(c0):
        def run():
            g = proj(OFF_G + c0, PROJ_COLS)
            gs_ref[:, c0:c0 + PROJ_COLS] = (g * _sigmoid(g)).astype(BF16)
        return run

    def gate_task(c0):
        def run():
            gt = proj(OFF_GT + c0, PROJ_COLS) + bg_ref[:, c0:c0 + PROJ_COLS]
            gt_ref[:, c0:c0 + PROJ_COLS] = _sigmoid(gt).astype(BF16)
        return run

    tasks = []
    for hd in range(RET_HEADS):
        tasks.append(rotary_task(OFF_Q, q_ref, hd, cos_q, sin_q))
        tasks.append(rotary_task(OFF_K, k_ref, hd, cos, sin))
    tasks += [v_task(c0) for c0 in range(0, RET_V_W, PROJ_COLS)]
    tasks += [g_task(c0) for c0 in range(0, RET_V_W, PROJ_COLS)]
    tasks += [gate_task(c0) for c0 in range(0, 2 * D_MODEL, PROJ_COLS)]

    for c0 in range(0, CONV_CH, PROJ_COLS):
        a = proj(OFF_CA + c0, PROJ_COLS)
        b = proj(OFF_CB + c0, PROJ_COLS)
        cpad_ref[CONV_HALO:CONV_HALO + rows, c0:c0 + PROJ_COLS] = a * _sigmoid(b)

    first = CONV_HALO - (CONV_WIDTH - 1)
    lane_groups = CONV_CH // LANES
    per_group = -(-len(tasks) // lane_groups)
    for grp in range(lane_groups):
        l0 = grp * LANES
        for r0 in range(0, rows, CONV_ROW_CHUNK):
            acc = None
            for res in range(SUBLANES):
                z = None
                for j in range(CONV_WIDTH):
                    off = first + j
                    if off % SUBLANES != res:
                        continue
                    term = cw_ref[j:j + 1, l0:l0 + LANES] * cpad_ref[r0 + off:r0 + off + CONV_ROW_CHUNK,
                                                                     l0:l0 + LANES]
                    z = term if z is None else z + term
                if z is not None:
                    acc = z if acc is None else acc + z
            y_ref[r0:r0 + CONV_ROW_CHUNK, l0:l0 + LANES] = acc + cb_ref[:, l0:l0 + LANES]
        for task in tasks[grp * per_group:(grp + 1) * per_group]:
            task()

    y = y_ref[...]
    mu = jnp.mean(y, axis=-1, keepdims=True)
    d = y - mu
    var = jnp.mean(d * d, axis=-1, keepdims=True)
    z = d * lax.rsqrt(var + LN_EPS) * lg_ref[...] + lb_ref[...]
    cs_ref[...] = (z * _sigmoid(z)).astype(BF16)

    cpad_ref[0:CONV_HALO, :] = cpad_ref[rows:rows + CONV_HALO, :]


def _inproj(h, layer, p, cos, sin):
    seq = h.shape[0]
    t = INPROJ_ROWS
    lyr = lambda *rest: (layer,) + rest
    return pl.pallas_call(
        functools.partial(_inproj_body, layer=layer),
        out_shape=(
            jax.ShapeDtypeStruct((seq, RET_QK_W), BF16),
            jax.ShapeDtypeStruct((seq, RET_QK_W), BF16),
            jax.ShapeDtypeStruct((seq, RET_V_W), BF16),
            jax.ShapeDtypeStruct((seq, RET_V_W), BF16),
            jax.ShapeDtypeStruct((seq, CONV_CH), BF16),
            jax.ShapeDtypeStruct((seq, 2 * D_MODEL), BF16),
        ),
        grid=(seq // t,),
        in_specs=[
            _rows(t, D_MODEL),
            _resident((None, 1, D_MODEL), lyr(0, 0)),
            UNBLOCKED,
            _rows(t, RET_QK_DIM // 2),
            _rows(t, RET_QK_DIM // 2),
            _resident((None, 1, 2 * D_MODEL), lyr(0, 0)),
            _resident((None, CONV_WIDTH, CONV_CH), lyr(0, 0)),
            _resident((None, 1, CONV_CH), lyr(0, 0)),
            _resident((None, 1, CONV_CH), lyr(0, 0)),
            _resident((None, 1, CONV_CH), lyr(0, 0)),
        ],
        out_specs=(
            _rows(t, RET_QK_W), _rows(t, RET_QK_W), _rows(t, RET_V_W), _rows(t, RET_V_W),
            _rows(t, CONV_CH), _rows(t, 2 * D_MODEL),
        ),
        scratch_shapes=[
            pltpu.VMEM((CONV_HALO + t, CONV_CH), F32),
            pltpu.VMEM((t, CONV_CH), F32),
            pltpu.VMEM((D_MODEL, IN_W), BF16),
            *_staging(IN_W, [D_MODEL]),
        ],
        compiler_params=_params("arbitrary"),
        name="inproj",
    )(h, p["norm_mix_g"], p["w_in"], cos, sin, p["b_gate"], p["conv_dw_w"], p["conv_dw_b"],
      p["conv_ln_g"], p["conv_ln_b"])


def _retention_body(q_ref, k_ref, v_ref, gs_ref, h_ref, cs_ref, gt_ref, dm_ref, dq_ref, dk_ref, db_ref,
                    gn_ref, bc_ref, wo_any, wc_any, wm_any, out_ref,
                    state_ref, gated_ref, wo_ref, wc_ref, wm_ref, w_stage, w_sem, *, layer):
    @pl.when(pl.program_id(0) == 0)
    def _():
        state_ref[...] = jnp.zeros(state_ref.shape, F32)
        _stage_weight(wo_any, layer, wo_ref, w_stage, w_sem)
        _stage_weight(wc_any, layer, wc_ref, w_stage, w_sem)
        _stage_weight(wm_any, layer, wm_ref, w_stage, w_sem)

    for r0 in range(0, q_ref.shape[0], RET_BLOCK):
        blk = slice(r0, r0 + RET_BLOCK)
        for hd in range(RET_HEADS):
            qk = slice(hd * RET_QK_DIM, (hd + 1) * RET_QK_DIM)
            vv = slice(hd * RET_V_DIM, (hd + 1) * RET_V_DIM)
            q = q_ref[blk, qk]
            k = k_ref[blk, qk]
            v = v_ref[blk, vv]
            s = _dot_nt(q, k) * dm_ref[hd]
            o = _dot(s.astype(BF16), v)
            st = state_ref[hd]
            cross = _dot(q, st.astype(BF16))
            o = o + cross * jnp.concatenate([dq_ref[hd]] * (RET_V_DIM // LANES), axis=-1)
            kd = k.astype(F32) * jnp.concatenate([dk_ref[hd]] * (RET_QK_DIM // LANES), axis=-1)
            state_ref[hd] = st * db_ref[hd] + _dot(kd.T.astype(BF16), v)
            mu = jnp.mean(o, axis=-1, keepdims=True)
            d = o - mu
            var = jnp.mean(d * d, axis=-1, keepdims=True)
            y = d * lax.rsqrt(var + LN_EPS) * gn_ref[:, vv]
            gated_ref[blk, vv] = (gs_ref[blk, vv].astype(F32) * y).astype(BF16)

    ya = _dot(gated_ref[...], wo_ref[...])
    yb = _dot(cs_ref[...], wc_ref[...]) + bc_ref[...]
    ga = gt_ref[:, :D_MODEL].astype(F32)
    gb = gt_ref[:, D_MODEL:].astype(F32)
    m = (ga * ya + gb * yb).astype(BF16)
    out_ref[...] = h_ref[...] + _dot(m, wm_ref[...])


def _retention(h, q, k, v, gs, cs, gt, layer, p, consts):
    seq = q.shape[0]
    t = RET_ROWS
    dmask, dq, dk, dblk = consts
    lyr = lambda *rest: (layer,) + rest
    return pl.pallas_call(
        functools.partial(_retention_body, layer=layer),
        out_shape=jax.ShapeDtypeStruct((seq, D_MODEL), F32),
        grid=(seq // t,),
        in_specs=[
            _rows(t, RET_QK_W), _rows(t, RET_QK_W), _rows(t, RET_V_W), _rows(t, RET_V_W),
            _rows(t, D_MODEL), _rows(t, CONV_CH), _rows(t, 2 * D_MODEL),
            _resident(dmask.shape, (0, 0, 0)),
            _resident(dq.shape, (0, 0, 0)),
            _resident(dk.shape, (0, 0, 0)),
            _resident(dblk.shape, (0, 0, 0)),
            _resident((None, 1, RET_V_W), lyr(0, 0)),
            _resident((None, 1, D_MODEL), lyr(0, 0)),
            UNBLOCKED, UNBLOCKED, UNBLOCKED,
        ],
        out_specs=_rows(t, D_MODEL),
        scratch_shapes=[
            pltpu.VMEM((RET_HEADS, RET_QK_DIM, RET_V_DIM), F32),
            pltpu.VMEM((t, RET_V_W), BF16),
            pltpu.VMEM((RET_V_W, D_MODEL), BF16),
            pltpu.VMEM((CONV_CH, D_MODEL), BF16),
            pltpu.VMEM((D_MODEL, D_MODEL), BF16),
            *_staging(D_MODEL, [RET_V_W, CONV_CH, D_MODEL]),
        ],
        compiler_params=_params("arbitrary"),
        name="retention_mix",
    )(q, k, v, gs, h, cs, gt, dmask, dq, dk, dblk, p["ret_gn_g"], p["b_conv_out"],
      p["w_ret_out"], p["w_conv_out"], p["w_mix_out"])


def _xprep_body(mem_ref, g_ref, wkv_ref, wq_ref, wo_ref, a_ref, b_ref):
    mn = _rms_norm(mem_ref[...], g_ref[...]).astype(BF16)
    kv = _dot(mn, wkv_ref[...].astype(BF16))
    for hd in range(X_HEADS):
        cols = slice(hd * X_HEAD_DIM, (hd + 1) * X_HEAD_DIM)
        k_h = kv[:, cols].astype(BF16)
        v_h = kv[:, D_MODEL + hd * X_HEAD_DIM:D_MODEL + (hd + 1) * X_HEAD_DIM].astype(BF16)
        a_ref[:, cols] = (_dot_nt(wq_ref[:, cols].astype(BF16), k_h) * (X_HEAD_DIM ** -0.5)).astype(BF16)
        b_ref[cols, :] = _dot(v_h, wo_ref[cols, :].astype(BF16)).astype(BF16)


def _xprep(mem, norm_g, w_xkv, w_xq, w_xo):
    depth = w_xkv.shape[0]
    mlen = mem.shape[0]
    assert mlen == X_HEAD_DIM
    per_layer = lambda *shape: pl.BlockSpec((None,) + shape, lambda l: (l,) + (0,) * len(shape))
    return pl.pallas_call(
        _xprep_body,
        out_shape=(jax.ShapeDtypeStruct((depth, D_MODEL, X_HEADS * mlen), BF16),
                   jax.ShapeDtypeStruct((depth, X_HEADS * mlen, D_MODEL), BF16)),
        grid=(depth,),
        in_specs=[
            _resident((mlen, D_MODEL), (0, 0)),
            per_layer(1, D_MODEL),
            per_layer(D_MODEL, 2 * D_MODEL),
            per_layer(D_MODEL, D_MODEL),
            per_layer(D_MODEL, D_MODEL),
        ],
        out_specs=(per_layer(D_MODEL, X_HEADS * mlen), per_layer(X_HEADS * mlen, D_MODEL)),
        compiler_params=_params("parallel"),
        name="xprep",
    )(mem, norm_g, w_xkv, w_xq, w_xo)


def _xattn_body(h_ref, g_ref, a_ref, b_ref, out_ref, p_ref):
    x = h_ref[...]
    hn = _rms_norm(x, g_ref[...]).astype(BF16)
    s = _dot(hn, a_ref[...])
    mlen = s.shape[1] // X_HEADS
    for hd in range(X_HEADS):
        cols = slice(hd * mlen, (hd + 1) * mlen)
        sh = s[:, cols]
        e = jnp.exp(sh - jnp.max(sh, axis=-1, keepdims=True))
        p_ref[:, cols] = (e * (1.0 / jnp.sum(e, axis=-1, keepdims=True))).astype(BF16)
    out_ref[...] = x + _dot(p_ref[...], b_ref[...])


def _xattn(h, xa, xb, layer, p):
    seq = h.shape[0]
    t = XATTN_ROWS
    lyr = lambda *rest: (layer,) + rest
    return pl.pallas_call(
        _xattn_body,
        out_shape=jax.ShapeDtypeStruct((seq, D_MODEL), F32),
        grid=(seq // t,),
        in_specs=[
            _rows(t, D_MODEL),
            _resident((None, 1, D_MODEL), lyr(0, 0)),
            _resident((None,) + xa.shape[1:], lyr(0, 0)),
            _resident((None,) + xb.shape[1:], lyr(0, 0)),
        ],
        out_specs=_rows(t, D_MODEL),
        scratch_shapes=[pltpu.VMEM((t, xa.shape[2]), BF16)],
        compiler_params=_params("parallel"),
        name="xattn",
    )(h, p["norm_xattn_g"], xa, xb)


def _ffn_body(h_ref, g_ref, wu_any, fw_ref, fb_ref, wd_any, fg_ref, out_ref, act_ref, prev_ref,
              wu_ref, wu_stage, wu_sem, wd_ref, wd_stage, wd_sem, *, layer, final_norm):
    rows = h_ref.shape[0]

    @pl.when(pl.program_id(0) == 0)
    def _():
        prev_ref[...] = jnp.zeros(prev_ref.shape, F32)
        _stage_weight(wu_any, layer, wu_ref, wu_stage, wu_sem)
        _stage_weight(wd_any, layer, wd_ref, wd_stage, wd_sem)

    x = h_ref[...]
    hn = _rms_norm(x, g_ref[...]).astype(BF16)
    row = lax.broadcasted_iota(jnp.int32, (rows, FFN_COLS), 0)
    for c0 in range(0, FFN_DIM, FFN_COLS):
        cols = slice(c0, c0 + FFN_COLS)
        val = _dot(hn, wu_ref[:, cols])
        gate = _dot(hn, wu_ref[:, FFN_DIM + c0:FFN_DIM + c0 + FFN_COLS])
        prev = prev_ref[:, cols]
        p1 = prev[SUBLANES - 1:SUBLANES, :]
        p2 = prev[SUBLANES - 2:SUBLANES - 1, :]
        g1 = jnp.where(row == 0, p1, pltpu.roll(gate, 1, axis=0))
        g2 = jnp.where(row == 0, p2, jnp.where(row == 1, p1, pltpu.roll(gate, 2, axis=0)))
        conv = fw_ref[0:1, cols] * g2 + fw_ref[1:2, cols] * g1 + fw_ref[2:3, cols] * gate + fb_ref[:, cols]
        act_ref[:, cols] = (conv * _sigmoid(conv) * val).astype(BF16)
        prev_ref[:, cols] = gate[rows - SUBLANES:, :]
    out = x + _dot(act_ref[...], wd_ref[...])
    if final_norm:
        out = _rms_norm(out, fg_ref[...])
    out_ref[...] = out


def _ffn(h, layer, p, final_g, final_norm):
    seq = h.shape[0]
    t = FFN_ROWS
    lyr = lambda *rest: (layer,) + rest
    return pl.pallas_call(
        functools.partial(_ffn_body, layer=layer, final_norm=final_norm),
        out_shape=jax.ShapeDtypeStruct((seq, D_MODEL), F32),
        grid=(seq // t,),
        in_specs=[
            _rows(t, D_MODEL),
            _resident((None, 1, D_MODEL), lyr(0, 0)),
            UNBLOCKED,
            _resident((None, FFN_CONV_WIDTH, FFN_DIM), lyr(0, 0)),
            _resident((None, 1, FFN_DIM), lyr(0, 0)),
            UNBLOCKED,
            _resident((1, D_MODEL), (0, 0)),
        ],
        out_specs=_rows(t, D_MODEL),
        scratch_shapes=[
            pltpu.VMEM((t, FFN_DIM), BF16),
            pltpu.VMEM((SUBLANES, FFN_DIM), F32),
            pltpu.VMEM((D_MODEL, 2 * FFN_DIM), BF16), *_staging(2 * FFN_DIM, [D_MODEL]),
            pltpu.VMEM((FFN_DIM, D_MODEL), BF16), *_staging(D_MODEL, [FFN_DIM]),
        ],
        compiler_params=_params("arbitrary"),
        name="ffn",
    )(h, p["norm_ffn_g"], p["w_up"], p["ffn_dw_w"], p["ffn_dw_b"], p["w_down"], final_g)


def _retention_constants(block):
    log_gamma = jnp.log(1.0 - jnp.power(2.0, -5.0 - jnp.arange(RET_HEADS, dtype=F32)))
    idx = jnp.arange(block, dtype=jnp.int32)
    n = idx[:, None]
    m = idx[None, :]
    same = (n // CHUNK) == (m // CHUNK)
    earlier = (m // CHUNK) < (n // CHUNK)
    dist = jnp.where(same, jnp.abs(n - m), n - m).astype(F32)
    dmask = jnp.where((same | earlier)[None], jnp.exp(log_gamma[:, None, None] * dist[None]), 0.0)
    r = jnp.arange(block, dtype=F32)
    dq = jnp.exp(log_gamma[:, None] * (r[None, :] + 1.0))
    dk = jnp.exp(log_gamma[:, None] * (block - 1.0 - r[None, :]))
    dq = jnp.broadcast_to(dq[:, :, None], (RET_HEADS, block, LANES))
    dk = jnp.broadcast_to(dk[:, :, None], (RET_HEADS, block, LANES))
    dblk = jnp.broadcast_to(jnp.exp(log_gamma * block)[:, None, None], (RET_HEADS, 1, RET_V_DIM))
    return dmask.astype(F32), dq, dk, dblk


def kernel(x, mem, positions, norm_mix_g, w_in, b_gate, ret_gn_g, w_ret_out, conv_dw_w, conv_dw_b,
           conv_ln_g, conv_ln_b, w_conv_out, b_conv_out, w_mix_out, norm_xattn_g, norm_mem_g, w_xq,
           w_xkv, w_xo, norm_ffn_g, w_up, ffn_dw_w, ffn_dw_b, w_down, norm_final_g):
    batch, seq, d_model = x.shape
    depth = w_in.shape[0]
    assert batch == 1 and d_model == D_MODEL and w_in.shape[2] == IN_W

    vec = lambda a: a.reshape(a.shape[0], 1, a.shape[1])
    p = {
        "norm_mix_g": vec(norm_mix_g), "w_in": w_in, "b_gate": vec(b_gate),
        "ret_gn_g": vec(ret_gn_g), "w_ret_out": w_ret_out,
        "conv_dw_w": conv_dw_w, "conv_dw_b": vec(conv_dw_b),
        "conv_ln_g": vec(conv_ln_g), "conv_ln_b": vec(conv_ln_b),
        "w_conv_out": w_conv_out, "b_conv_out": vec(b_conv_out),
        "w_mix_out": w_mix_out,
        "norm_xattn_g": vec(norm_xattn_g),
        "norm_ffn_g": vec(norm_ffn_g), "w_up": w_up,
        "ffn_dw_w": ffn_dw_w, "ffn_dw_b": vec(ffn_dw_b), "w_down": w_down,
    }

    inv_freq = 1.0 / (ROPE_THETA ** (jnp.arange(0, RET_QK_DIM, 2, dtype=F32) / RET_QK_DIM))
    cos, sin = _rope_tables(positions.astype(F32).reshape(seq, 1), inv_freq.reshape(1, -1))
    consts = _retention_constants(RET_BLOCK)
    xa, xb = _xprep(mem[0], vec(norm_mem_g), w_xkv, w_xq, w_xo)
    final_g = norm_final_g.reshape(1, d_model)

    h = x[0]
    for layer in range(depth):
        q, k, v, gs, cs, gt = _inproj(h, layer, p, cos, sin)
        h = _retention(h, q, k, v, gs, cs, gt, layer, p, consts)
        h = _xattn(h, xa, xb, layer, p)
        h = _ffn(h, layer, p, final_g, final_norm=(layer == depth - 1))
    return h[None]
```

```python
import functools

import jax
import jax.numpy as jnp
from jax import lax
from jax.experimental import pallas as pl
from jax.experimental.pallas import tpu as pltpu

F32 = jnp.float32
BF16 = jnp.bfloat16

D_MODEL = 1024
CHUNK = 64
RET_HEADS = 4
RET_QK_DIM = 256
RET_V_DIM = 512
RET_QK_W = RET_HEADS * RET_QK_DIM
RET_V_W = RET_HEADS * RET_V_DIM
ROPE_THETA = 10000.0
CONV_CH = D_MODEL
CONV_WIDTH = 31
X_HEADS = 4
X_HEAD_DIM = D_MODEL // X_HEADS
FFN_DIM = 2816
FFN_CONV_WIDTH = 3
RMS_EPS = 1e-6
LN_EPS = 1e-5
NEG_LOG2_E = -1.4426950408889634

OFF_Q = 0
OFF_K = OFF_Q + RET_QK_W
OFF_V = OFF_K + RET_QK_W
OFF_G = OFF_V + RET_V_W
OFF_CA = OFF_G + RET_V_W
OFF_CB = OFF_CA + CONV_CH
OFF_GT = OFF_CB + CONV_CH
IN_W = OFF_GT + 2 * D_MODEL

LANES = 128
SUBLANES = 8
BF16_ROWS = 16
VMEM_LIMIT = 56 * 1024 * 1024

ROPE_ROWS = 1024
INPROJ_ROWS = 256
RET_BLOCK = 256
RET_ROWS = 512
XATTN_ROWS = 1024
FFN_ROWS = 512
FFN_COLS = 256
CONV_HALO = 32
CONV_ROW_CHUNK = 64
PROJ_COLS = 256
STAGE_SLOT_BYTES = 11 * 512 * 1024
STAGE_SLOTS = 2


def _resident(block_shape, index):
    return pl.BlockSpec(block_shape, lambda i: index, pipeline_mode=pl.Buffered(1))


def _rows(block_rows, width):
    return pl.BlockSpec((block_rows, width), lambda i: (i, 0))


def _params(semantics):
    return pltpu.CompilerParams(dimension_semantics=(semantics,), vmem_limit_bytes=VMEM_LIMIT)


UNBLOCKED = pl.BlockSpec(memory_space=pl.ANY)


def _stage_rows(n, ks):
    fits = [r for r in range(BF16_ROWS, min(ks) + 1, BF16_ROWS)
            if all(k % r == 0 for k in ks) and r * n * 4 <= STAGE_SLOT_BYTES]
    return max(fits)


def _staging(n, ks):
    return [pltpu.VMEM((STAGE_SLOTS, _stage_rows(n, ks), n), F32), pltpu.SemaphoreType.DMA((STAGE_SLOTS,))]


def _stage_weight(w_any, layer, w_vmem, stage, sem):
    slots, rows, _ = stage.shape
    chunks = w_vmem.shape[0] // rows

    def copy(c):
        return pltpu.make_async_copy(w_any.at[layer, pl.ds(c * rows, rows), :], stage.at[c % slots],
                                     sem.at[c % slots])

    for c in range(min(slots, chunks)):
        copy(c).start(priority=c % slots)
    for c in range(chunks):
        copy(c).wait()
        w_vmem[c * rows:(c + 1) * rows, :] = stage[c % slots].astype(BF16)
        if c + slots < chunks:
            copy(c + slots).start(priority=c % slots)


def _rms_norm(x, g):
    ms = jnp.mean(x * x, axis=-1, keepdims=True)
    return x * lax.rsqrt(ms + RMS_EPS) * g


def _sigmoid(x):
    return 1.0 / (1.0 + jnp.exp2(x * NEG_LOG2_E))


def _dot(a, b):
    return jnp.dot(a, b, preferred_element_type=F32)


def _dot_nt(a, b):
    return lax.dot_general(a, b, (((1,), (1,)), ((), ())), preferred_element_type=F32)


def _rope_body(pos_ref, invf_ref, cos_ref, sin_ref):
    ang = pos_ref[...] * invf_ref[...]
    cos_ref[...] = jnp.cos(ang)
    sin_ref[...] = jnp.sin(ang)


def _rope_tables(pos, inv_freq):
    seq = pos.shape[0]
    half = inv_freq.shape[1]
    return pl.pallas_call(
        _rope_body,
        out_shape=(jax.ShapeDtypeStruct((seq, half), F32),) * 2,
        grid=(seq // ROPE_ROWS,),
        in_specs=[_rows(ROPE_ROWS, 1), _resident((1, half), (0, 0))],
        out_specs=(_rows(ROPE_ROWS, half),) * 2,
        compiler_params=_params("parallel"),
        name="rope_tables",
    )(pos, inv_freq)


def _inproj_body(h_ref, ng_ref, w_any, cos_ref, sin_ref, bg_ref, cw_ref, cb_ref, lg_ref, lb_ref,
                 q_ref, k_ref, v_ref, gs_ref, cs_ref, gt_ref, cpad_ref, y_ref, w_ref, w_stage, w_sem,
                 *, layer):
    rows = h_ref.shape[0]
    half = RET_QK_DIM // 2

    @pl.when(pl.program_id(0) == 0)
    def _():
        cpad_ref[0:CONV_HALO, :] = jnp.zeros((CONV_HALO, CONV_CH), F32)
        y_ref[...] = jnp.zeros(y_ref.shape, F32)
        _stage_weight(w_any, layer, w_ref, w_stage, w_sem)

    y = y_ref[...]
    mu = jnp.mean(y, axis=-1, keepdims=True)
    d = y - mu
    var = jnp.mean(d * d, axis=-1, keepdims=True)
    z = d * lax.rsqrt(var + LN_EPS) * lg_ref[...] + lb_ref[...]
    cs_ref[...] = (z * _sigmoid(z)).astype(BF16)

    u = _rms_norm(h_ref[...], ng_ref[...]).astype(BF16)
    cos = cos_ref[...]
    sin = sin_ref[...]
    q_scale = RET_QK_DIM ** -0.5
    cos_q = cos * q_scale
    sin_q = sin * q_scale

    def proj(col, width):
        return _dot(u, w_ref[:, col:col + width])

    def rotary_task(base, out_ref, hd, cs_, sn_):
        def run():
            p = proj(base + hd * RET_QK_DIM, RET_QK_DIM)
            x1 = p[:, :half]
            x2 = p[:, half:]
            c0 = hd * RET_QK_DIM
            out_ref[:, c0:c0 + half] = (x1 * cs_ - x2 * sn_).astype(BF16)
            out_ref[:, c0 + half:c0 + RET_QK_DIM] = (x2 * cs_ + x1 * sn_).astype(BF16)
        return run

    def v_task(c0):
        def run():
            v_ref[:, c0:c0 + PROJ_COLS] = proj(OFF_V + c0, PROJ_COLS).astype(BF16)
        return run

    def g_task(c0):
        def run():
            g = proj(OFF_G + c0, PROJ_COLS)
            gs_ref[:, c0:c0 + PROJ_COLS] = (g * _sigmoid(g)).astype(BF16)
        return run

    def gate_task(c0):
        def run():
            gt = proj(OFF_GT + c0, PROJ_COLS) + bg_ref[:, c0:c0 + PROJ_COLS]
            gt_ref[:, c0:c0 + PROJ_COLS] = _sigmoid(gt).astype(BF16)
        return run

    tasks = []
    for hd in range(RET_HEADS):
        tasks.append(rotary_task(OFF_Q, q_ref, hd, cos_q, sin_q))
        tasks.append(rotary_task(OFF_K, k_ref, hd, cos, sin))
    tasks += [v_task(c0) for c0 in range(0, RET_V_W, PROJ_COLS)]
    tasks += [g_task(c0) for c0 in range(0, RET_V_W, PROJ_COLS)]
    tasks += [gate_task(c0) for c0 in range(0, 2 * D_MODEL, PROJ_COLS)]

    for c0 in range(0, CONV_CH, PROJ_COLS):
        a = proj(OFF_CA + c0, PROJ_COLS)
        b = proj(OFF_CB + c0, PROJ_COLS)
        cpad_ref[CONV_HALO:CONV_HALO + rows, c0:c0 + PROJ_COLS] = a * _sigmoid(b)

    first = CONV_HALO - (CONV_WIDTH - 1)
    lane_groups = CONV_CH // LANES
    per_group = -(-len(tasks) // lane_groups)
    for grp in range(lane_groups):
        l0 = grp * LANES
        for r0 in range(0, rows, CONV_ROW_CHUNK):
            acc = None
            for res in range(SUBLANES):
                z = None
                for j in range(CONV_WIDTH):
                    off = first + j
                    if off % SUBLANES != res:
                        continue
                    term = cw_ref[j:j + 1, l0:l0 + LANES] * cpad_ref[r0 + off:r0 + off + CONV_ROW_CHUNK,
                                                                     l0:l0 + LANES]
                    z = term if z is None else z + term
                if z is not None:
                    acc = z if acc is None else acc + z
            y_ref[r0:r0 + CONV_ROW_CHUNK, l0:l0 + LANES] = acc + cb_ref[:, l0:l0 + LANES]
        for task in tasks[grp * per_group:(grp + 1) * per_group]:
            task()

    cpad_ref[0:CONV_HALO, :] = cpad_ref[rows:rows + CONV_HALO, :]


def _inproj(h, layer, p, cos, sin):
    seq = h.shape[0]
    t = INPROJ_ROWS
    tiles = seq // t
    lyr = lambda *rest: (layer,) + rest
    cur = lambda width: pl.BlockSpec((t, width), lambda i: (jnp.minimum(i, tiles - 1), 0))
    return pl.pallas_call(
        functools.partial(_inproj_body, layer=layer),
        out_shape=(
            jax.ShapeDtypeStruct((seq, RET_QK_W), BF16),
            jax.ShapeDtypeStruct((seq, RET_QK_W), BF16),
            jax.ShapeDtypeStruct((seq, RET_V_W), BF16),
            jax.ShapeDtypeStruct((seq, RET_V_W), BF16),
            jax.ShapeDtypeStruct((seq, CONV_CH), BF16),
            jax.ShapeDtypeStruct((seq, 2 * D_MODEL), BF16),
        ),
        grid=(tiles + 1,),
        in_specs=[
            cur(D_MODEL),
            _resident((None, 1, D_MODEL), lyr(0, 0)),
            UNBLOCKED,
            cur(RET_QK_DIM // 2),
            cur(RET_QK_DIM // 2),
            _resident((None, 1, 2 * D_MODEL), lyr(0, 0)),
            _resident((None, CONV_WIDTH, CONV_CH), lyr(0, 0)),
            _resident((None, 1, CONV_CH), lyr(0, 0)),
            _resident((None, 1, CONV_CH), lyr(0, 0)),
            _resident((None, 1, CONV_CH), lyr(0, 0)),
        ],
        out_specs=(
            cur(RET_QK_W), cur(RET_QK_W), cur(RET_V_W), cur(RET_V_W),
            pl.BlockSpec((t, CONV_CH), lambda i: (jnp.maximum(i - 1, 0), 0)),
            cur(2 * D_MODEL),
        ),
        scratch_shapes=[
            pltpu.VMEM((CONV_HALO + t, CONV_CH), F32),
            pltpu.VMEM((t, CONV_CH), F32),
            pltpu.VMEM((D_MODEL, IN_W), BF16),
            *_staging(IN_W, [D_MODEL]),
        ],
        compiler_params=_params("arbitrary"),
        name="inproj",
    )(h, p["norm_mix_g"], p["w_in"], cos, sin, p["b_gate"], p["conv_dw_w"], p["conv_dw_b"],
      p["conv_ln_g"], p["conv_ln_b"])


def _retention_body(q_ref, k_ref, v_ref, gs_ref, h_ref, cs_ref, gt_ref, dm_ref, dq_ref, dk_ref, db_ref,
                    gn_ref, bc_ref, wo_any, wc_any, wm_any, out_ref,
                    state_ref, gated_ref, wo_ref, wc_ref, wm_ref, w_stage, w_sem, *, layer):
    @pl.when(pl.program_id(0) == 0)
    def _():
        state_ref[...] = jnp.zeros(state_ref.shape, F32)
        _stage_weight(wo_any, layer, wo_ref, w_stage, w_sem)
        _stage_weight(wc_any, layer, wc_ref, w_stage, w_sem)
        _stage_weight(wm_any, layer, wm_ref, w_stage, w_sem)

    for r0 in range(0, q_ref.shape[0], RET_BLOCK):
        blk = slice(r0, r0 + RET_BLOCK)
        for hd in range(RET_HEADS):
            qk = slice(hd * RET_QK_DIM, (hd + 1) * RET_QK_DIM)
            vv = slice(hd * RET_V_DIM, (hd + 1) * RET_V_DIM)
            q = q_ref[blk, qk]
            k = k_ref[blk, qk]
            v = v_ref[blk, vv]
            s = _dot_nt(q, k) * dm_ref[hd]
            o = _dot(s.astype(BF16), v)
            st = state_ref[hd]
            cross = _dot(q, st.astype(BF16))
            o = o + cross * jnp.concatenate([dq_ref[hd]] * (RET_V_DIM // LANES), axis=-1)
            kd = k.astype(F32) * jnp.concatenate([dk_ref[hd]] * (RET_QK_DIM // LANES), axis=-1)
            state_ref[hd] = st * db_ref[hd] + _dot(kd.T.astype(BF16), v)
            mu = jnp.mean(o, axis=-1, keepdims=True)
            d = o - mu
            var = jnp.mean(d * d, axis=-1, keepdims=True)
            y = d * lax.rsqrt(var + LN_EPS) * gn_ref[:, vv]
            gated_ref[blk, vv] = (gs_ref[blk, vv].astype(F32) * y).astype(BF16)

    ya = _dot(gated_ref[...], wo_ref[...])
    yb = _dot(cs_ref[...], wc_ref[...]) + bc_ref[...]
    ga = gt_ref[:, :D_MODEL].astype(F32)
    gb = gt_ref[:, D_MODEL:].astype(F32)
    m = (ga * ya + gb * yb).astype(BF16)
    out_ref[...] = h_ref[...] + _dot(m, wm_ref[...])


def _retention(h, q, k, v, gs, cs, gt, layer, p, consts):
    seq = q.shape[0]
    t = RET_ROWS
    dmask, dq, dk, dblk = consts
    lyr = lambda *rest: (layer,) + rest
    return pl.pallas_call(
        functools.partial(_retention_body, layer=layer),
        out_shape=jax.ShapeDtypeStruct((seq, D_MODEL), F32),
        grid=(seq // t,),
        in_specs=[
            _rows(t, RET_QK_W), _rows(t, RET_QK_W), _rows(t, RET_V_W), _rows(t, RET_V_W),
            _rows(t, D_MODEL), _rows(t, CONV_CH), _rows(t, 2 * D_MODEL),
            _resident(dmask.shape, (0, 0, 0)),
            _resident(dq.shape, (0, 0, 0)),
            _resident(dk.shape, (0, 0, 0)),
            _resident(dblk.shape, (0, 0, 0)),
            _resident((None, 1, RET_V_W), lyr(0, 0)),
            _resident((None, 1, D_MODEL), lyr(0, 0)),
            UNBLOCKED, UNBLOCKED, UNBLOCKED,
        ],
        out_specs=_rows(t, D_MODEL),
        scratch_shapes=[
            pltpu.VMEM((RET_HEADS, RET_QK_DIM, RET_V_DIM), F32),
            pltpu.VMEM((t, RET_V_W), BF16),
            pltpu.VMEM((RET_V_W, D_MODEL), BF16),
            pltpu.VMEM((CONV_CH, D_MODEL), BF16),
            pltpu.VMEM((D_MODEL, D_MODEL), BF16),
            *_staging(D_MODEL, [RET_V_W, CONV_CH, D_MODEL]),
        ],
        compiler_params=_params("arbitrary"),
        name="retention_mix",
    )(q, k, v, gs, h, cs, gt, dmask, dq, dk, dblk, p["ret_gn_g"], p["b_conv_out"],
      p["w_ret_out"], p["w_conv_out"], p["w_mix_out"])


def _xprep_body(mem_ref, g_ref, wkv_ref, wq_ref, wo_ref, a_ref, b_ref):
    mn = _rms_norm(mem_ref[...], g_ref[...]).astype(BF16)
    kv = _dot(mn, wkv_ref[...].astype(BF16))
    for hd in range(X_HEADS):
        cols = slice(hd * X_HEAD_DIM, (hd + 1) * X_HEAD_DIM)
        k_h = kv[:, cols].astype(BF16)
        v_h = kv[:, D_MODEL + hd * X_HEAD_DIM:D_MODEL + (hd + 1) * X_HEAD_DIM].astype(BF16)
        a_ref[:, cols] = (_dot_nt(wq_ref[:, cols].astype(BF16), k_h) * (X_HEAD_DIM ** -0.5)).astype(BF16)
        b_ref[cols, :] = _dot(v_h, wo_ref[cols, :].astype(BF16)).astype(BF16)


def _xprep(mem, norm_g, w_xkv, w_xq, w_xo):
    depth = w_xkv.shape[0]
    mlen = mem.shape[0]
    assert mlen == X_HEAD_DIM
    per_layer = lambda *shape: pl.BlockSpec((None,) + shape, lambda l: (l,) + (0,) * len(shape))
    return pl.pallas_call(
        _xprep_body,
        out_shape=(jax.ShapeDtypeStruct((depth, D_MODEL, X_HEADS * mlen), BF16),
                   jax.ShapeDtypeStruct((depth, X_HEADS * mlen, D_MODEL), BF16)),
        grid=(depth,),
        in_specs=[
            _resident((mlen, D_MODEL), (0, 0)),
            per_layer(1, D_MODEL),
            per_layer(D_MODEL, 2 * D_MODEL),
            per_layer(D_MODEL, D_MODEL),
            per_layer(D_MODEL, D_MODEL),
        ],
        out_specs=(per_layer(D_MODEL, X_HEADS * mlen), per_layer(X_HEADS * mlen, D_MODEL)),
        compiler_params=_params("parallel"),
        name="xprep",
    )(mem, norm_g, w_xkv, w_xq, w_xo)


def _xattn_body(h_ref, g_ref, a_ref, b_ref, out_ref, p_ref):
    x = h_ref[...]
    hn = _rms_norm(x, g_ref[...]).astype(BF16)
    s = _dot(hn, a_ref[...])
    mlen = s.shape[1] // X_HEADS
    for hd in range(X_HEADS):
        cols = slice(hd * mlen, (hd + 1) * mlen)
        sh = s[:, cols]
        e = jnp.exp(sh - jnp.max(sh, axis=-1, keepdims=True))
        p_ref[:, cols] = (e * (1.0 / jnp.sum(e, axis=-1, keepdims=True))).astype(BF16)
    out_ref[...] = x + _dot(p_ref[...], b_ref[...])


def _xattn(h, xa, xb, layer, p):
    seq = h.shape[0]
    t = XATTN_ROWS
    lyr = lambda *rest: (layer,) + rest
    return pl.pallas_call(
        _xattn_body,
        out_shape=jax.ShapeDtypeStruct((seq, D_MODEL), F32),
        grid=(seq // t,),
        in_specs=[
            _rows(t, D_MODEL),
            _resident((None, 1, D_MODEL), lyr(0, 0)),
            _resident((None,) + xa.shape[1:], lyr(0, 0)),
            _resident((None,) + xb.shape[1:], lyr(0, 0)),
        ],
        out_specs=_rows(t, D_MODEL),
        scratch_shapes=[pltpu.VMEM((t, xa.shape[2]), BF16)],
        compiler_params=_params("parallel"),
        name="xattn",
    )(h, p["norm_xattn_g"], xa, xb)


def _ffn_body(h_ref, g_ref, wu_any, fw_ref, fb_ref, wd_any, fg_ref, out_ref, act_ref, prev_ref,
              wu_ref, wu_stage, wu_sem, wd_ref, wd_stage, wd_sem, *, layer, final_norm):
    rows = h_ref.shape[0]

    @pl.when(pl.program_id(0) == 0)
    def _():
        prev_ref[...] = jnp.zeros(prev_ref.shape, F32)
        _stage_weight(wu_any, layer, wu_ref, wu_stage, wu_sem)
        _stage_weight(wd_any, layer, wd_ref, wd_stage, wd_sem)

    x = h_ref[...]
    hn = _rms_norm(x, g_ref[...]).astype(BF16)
    row = lax.broadcasted_iota(jnp.int32, (rows, FFN_COLS), 0)
    for c0 in range(0, FFN_DIM, FFN_COLS):
        cols = slice(c0, c0 + FFN_COLS)
        val = _dot(hn, wu_ref[:, cols])
        gate = _dot(hn, wu_ref[:, FFN_DIM + c0:FFN_DIM + c0 + FFN_COLS])
        prev = prev_ref[:, cols]
        p1 = prev[SUBLANES - 1:SUBLANES, :]
        p2 = prev[SUBLANES - 2:SUBLANES - 1, :]
        g1 = jnp.where(row == 0, p1, pltpu.roll(gate, 1, axis=0))
        g2 = jnp.where(row == 0, p2, jnp.where(row == 1, p1, pltpu.roll(gate, 2, axis=0)))
        conv = fw_ref[0:1, cols] * g2 + fw_ref[1:2, cols] * g1 + fw_ref[2:3, cols] * gate + fb_ref[:, cols]
        act_ref[:, cols] = (conv * _sigmoid(conv) * val).astype(BF16)
        prev_ref[:, cols] = gate[rows - SUBLANES:, :]
    out = x + _dot(act_ref[...], wd_ref[...])
    if final_norm:
        out = _rms_norm(out, fg_ref[...])
    out_ref[...] = out


def _ffn(h, layer, p, final_g, final_norm):
    seq = h.shape[0]
    t = FFN_ROWS
    lyr = lambda *rest: (layer,) + rest
    return pl.pallas_call(
        functools.partial(_ffn_body, layer=layer, final_norm=final_norm),
        out_shape=jax.ShapeDtypeStruct((seq, D_MODEL), F32),
        grid=(seq // t,),
        in_specs=[
            _rows(t, D_MODEL),
            _resident((None, 1, D_MODEL), lyr(0, 0)),
            UNBLOCKED,
            _resident((None, FFN_CONV_WIDTH, FFN_DIM), lyr(0, 0)),
            _resident((None, 1, FFN_DIM), lyr(0, 0)),
            UNBLOCKED,
            _resident((1, D_MODEL), (0, 0)),
        ],
        out_specs=_rows(t, D_MODEL),
        scratch_shapes=[
            pltpu.VMEM((t, FFN_DIM), BF16),
            pltpu.VMEM((SUBLANES, FFN_DIM), F32),
            pltpu.VMEM((D_MODEL, 2 * FFN_DIM), BF16), *_staging(2 * FFN_DIM, [D_MODEL]),
            pltpu.VMEM((FFN_DIM, D_MODEL), BF16), *_staging(D_MODEL, [FFN_DIM]),
        ],
        compiler_params=_params("arbitrary"),
        name="ffn",
    )(h, p["norm_ffn_g"], p["w_up"], p["ffn_dw_w"], p["ffn_dw_b"], p["w_down"], final_g)


def _retention_constants(block):
    log_gamma = jnp.log(1.0 - jnp.power(2.0, -5.0 - jnp.arange(RET_HEADS, dtype=F32)))
    idx = jnp.arange(block, dtype=jnp.int32)
    n = idx[:, None]
    m = idx[None, :]
    same = (n // CHUNK) == (m // CHUNK)
    earlier = (m // CHUNK) < (n // CHUNK)
    dist = jnp.where(same, jnp.abs(n - m), n - m).astype(F32)
    dmask = jnp.where((same | earlier)[None], jnp.exp(log_gamma[:, None, None] * dist[None]), 0.0)
    r = jnp.arange(block, dtype=F32)
    dq = jnp.exp(log_gamma[:, None] * (r[None, :] + 1.0))
    dk = jnp.exp(log_gamma[:, None] * (block - 1.0 - r[None, :]))
    dq = jnp.broadcast_to(dq[:, :, None], (RET_HEADS, block, LANES))
    dk = jnp.broadcast_to(dk[:, :, None], (RET_HEADS, block, LANES))
    dblk = jnp.broadcast_to(jnp.exp(log_gamma * block)[:, None, None], (RET_HEADS, 1, RET_V_DIM))
    return dmask.astype(F32), dq, dk, dblk


def kernel(x, mem, positions, norm_mix_g, w_in, b_gate, ret_gn_g, w_ret_out, conv_dw_w, conv_dw_b,
           conv_ln_g, conv_ln_b, w_conv_out, b_conv_out, w_mix_out, norm_xattn_g, norm_mem_g, w_xq,
           w_xkv, w_xo, norm_ffn_g, w_up, ffn_dw_w, ffn_dw_b, w_down, norm_final_g):
    batch, seq, d_model = x.shape
    depth = w_in.shape[0]
    assert batch == 1 and d_model == D_MODEL and w_in.shape[2] == IN_W

    vec = lambda a: a.reshape(a.shape[0], 1, a.shape[1])
    p = {
        "norm_mix_g": vec(norm_mix_g), "w_in": w_in, "b_gate": vec(b_gate),
        "ret_gn_g": vec(ret_gn_g), "w_ret_out": w_ret_out,
        "conv_dw_w": conv_dw_w, "conv_dw_b": vec(conv_dw_b),
        "conv_ln_g": vec(conv_ln_g), "conv_ln_b": vec(conv_ln_b),
        "w_conv_out": w_conv_out, "b_conv_out": vec(b_conv_out),
        "w_mix_out": w_mix_out,
        "norm_xattn_g": vec(norm_xattn_g),
        "norm_ffn_g": vec(norm_ffn_g), "w_up": w_up,
        "ffn_dw_w": ffn_dw_w, "ffn_dw_b": vec(ffn_dw_b), "w_down": w_down,
    }

    inv_freq = 1.0 / (ROPE_THETA ** (jnp.arange(0, RET_QK_DIM, 2, dtype=F32) / RET_QK_DIM))
    cos, sin = _rope_tables(positions.astype(F32).reshape(seq, 1), inv_freq.reshape(1, -1))
    consts = _retention_constants(RET_BLOCK)
    xa, xb = _xprep(mem[0], vec(norm_mem_g), w_xkv, w_xq, w_xo)
    final_g = norm_final_g.reshape(1, d_model)

    h = x[0]
    for layer in range(depth):
        q, k, v, gs, cs, gt = _inproj(h, layer, p, cos, sin)
        h = _retention(h, q, k, v, gs, cs, gt, layer, p, consts)
        h = _xattn(h, xa, xb, layer, p)
        h = _ffn(h, layer, p, final_g, final_norm=(layer == depth - 1))
    return h[None]
```

```python
import functools

import jax
import jax.numpy as jnp
from jax import lax
from jax.experimental import pallas as pl
from jax.experimental.pallas import tpu as pltpu

F32 = jnp.float32
BF16 = jnp.bfloat16

D_MODEL = 1024
CHUNK = 64
RET_HEADS = 4
RET_QK_DIM = 256
RET_V_DIM = 512
RET_QK_W = RET_HEADS * RET_QK_DIM
RET_V_W = RET_HEADS * RET_V_DIM
ROPE_THETA = 10000.0
CONV_CH = D_MODEL
CONV_WIDTH = 31
X_HEADS = 4
X_HEAD_DIM = D_MODEL // X_HEADS
FFN_DIM = 2816
FFN_CONV_WIDTH = 3
RMS_EPS = 1e-6
LN_EPS = 1e-5
NEG_LOG2_E = -1.4426950408889634

OFF_Q = 0
OFF_K = OFF_Q + RET_QK_W
OFF_V = OFF_K + RET_QK_W
OFF_G = OFF_V + RET_V_W
OFF_CA = OFF_G + RET_V_W
OFF_CB = OFF_CA + CONV_CH
OFF_GT = OFF_CB + CONV_CH
IN_W = OFF_GT + 2 * D_MODEL

ACT_Q = 0
ACT_K = ACT_Q + RET_QK_W
ACT_V = ACT_K + RET_QK_W
ACT_GS = ACT_V + RET_V_W
ACT_GT = ACT_GS + RET_V_W
ACT_W = ACT_GT + 2 * D_MODEL

LANES = 128
SUBLANES = 8
BF16_ROWS = 16
VMEM_LIMIT = 56 * 1024 * 1024

ROPE_ROWS = 1024
INPROJ_ROWS = 256
RET_BLOCK = 256
RET_ROWS = 512
XATTN_ROWS = 1024
FFN_ROWS = 512
FFN_COLS = 256
CONV_HALO = 32
CONV_ROW_CHUNK = 64
PROJ_COLS = 256
STAGE_SLOT_BYTES = 11 * 512 * 1024
STAGE_SLOTS = 2


def _resident(block_shape, index):
    return pl.BlockSpec(block_shape, lambda i: index, pipeline_mode=pl.Buffered(1))


def _rows(block_rows, width):
    return pl.BlockSpec((block_rows, width), lambda i: (i, 0))


def _params(semantics):
    return pltpu.CompilerParams(dimension_semantics=(semantics,), vmem_limit_bytes=VMEM_LIMIT)


UNBLOCKED = pl.BlockSpec(memory_space=pl.ANY)


def _stage_rows(n, ks):
    fits = [r for r in range(BF16_ROWS, min(ks) + 1, BF16_ROWS)
            if all(k % r == 0 for k in ks) and r * n * 4 <= STAGE_SLOT_BYTES]
    return max(fits)


def _staging(n, ks):
    return [pltpu.VMEM((STAGE_SLOTS, _stage_rows(n, ks), n), F32), pltpu.SemaphoreType.DMA((STAGE_SLOTS,))]


def _stage_weight(w_any, layer, w_vmem, stage, sem):
    slots, rows, _ = stage.shape
    chunks = w_vmem.shape[0] // rows

    def copy(c):
        return pltpu.make_async_copy(w_any.at[layer, pl.ds(c * rows, rows), :], stage.at[c % slots],
                                     sem.at[c % slots])

    for c in range(min(slots, chunks)):
        copy(c).start(priority=c % slots)
    for c in range(chunks):
        copy(c).wait()
        w_vmem[c * rows:(c + 1) * rows, :] = stage[c % slots].astype(BF16)
        if c + slots < chunks:
            copy(c + slots).start(priority=c % slots)


def _rms_norm(x, g):
    ms = jnp.mean(x * x, axis=-1, keepdims=True)
    return x * lax.rsqrt(ms + RMS_EPS) * g


def _sigmoid(x):
    return 1.0 / (1.0 + jnp.exp2(x * NEG_LOG2_E))


def _dot(a, b):
    return jnp.dot(a, b, preferred_element_type=F32)


def _dot_nt(a, b):
    return lax.dot_general(a, b, (((1,), (1,)), ((), ())), preferred_element_type=F32)


def _rope_body(pos_ref, invf_ref, tab_ref):
    ang = pos_ref[...] * invf_ref[...]
    half = ang.shape[1]
    tab_ref[:, :half] = jnp.cos(ang)
    tab_ref[:, half:] = jnp.sin(ang)


def _rope_tables(pos, inv_freq):
    seq = pos.shape[0]
    half = inv_freq.shape[1]
    return pl.pallas_call(
        _rope_body,
        out_shape=jax.ShapeDtypeStruct((seq, 2 * half), F32),
        grid=(seq // ROPE_ROWS,),
        in_specs=[_rows(ROPE_ROWS, 1), _resident((1, half), (0, 0))],
        out_specs=_rows(ROPE_ROWS, 2 * half),
        compiler_params=_params("parallel"),
        name="rope_tables",
    )(pos, inv_freq)


def _inproj_body(h_ref, ng_ref, w_any, rope_ref, bg_ref, cw_ref, cb_ref, lg_ref, lb_ref,
                 act_ref, cs_ref, cpad_ref, y_ref, w_ref, w_stage, w_sem, *, layer):
    rows = h_ref.shape[0]
    half = RET_QK_DIM // 2

    @pl.when(pl.program_id(0) == 0)
    def _():
        cpad_ref[0:CONV_HALO, :] = jnp.zeros((CONV_HALO, CONV_CH), F32)
        y_ref[...] = jnp.zeros(y_ref.shape, F32)
        _stage_weight(w_any, layer, w_ref, w_stage, w_sem)

    y = y_ref[...]
    mu = jnp.mean(y, axis=-1, keepdims=True)
    d = y - mu
    var = jnp.mean(d * d, axis=-1, keepdims=True)
    z = d * lax.rsqrt(var + LN_EPS) * lg_ref[...] + lb_ref[...]
    cs_ref[...] = (z * _sigmoid(z)).astype(BF16)

    u = _rms_norm(h_ref[...], ng_ref[...]).astype(BF16)
    cos = rope_ref[:, :half]
    sin = rope_ref[:, half:]
    q_scale = RET_QK_DIM ** -0.5
    cos_q = cos * q_scale
    sin_q = sin * q_scale

    def proj(col, width):
        return _dot(u, w_ref[:, col:col + width])

    def rotary_task(base, out, hd, cs_, sn_):
        def run():
            p = proj(base + hd * RET_QK_DIM, RET_QK_DIM)
            x1 = p[:, :half]
            x2 = p[:, half:]
            c0 = out + hd * RET_QK_DIM
            act_ref[:, c0:c0 + half] = (x1 * cs_ - x2 * sn_).astype(BF16)
            act_ref[:, c0 + half:c0 + RET_QK_DIM] = (x2 * cs_ + x1 * sn_).astype(BF16)
        return run

    def v_task(c0):
        def run():
            act_ref[:, ACT_V + c0:ACT_V + c0 + PROJ_COLS] = proj(OFF_V + c0, PROJ_COLS).astype(BF16)
        return run

    def g_task(c0):
        def run():
            g = proj(OFF_G + c0, PROJ_COLS)
            act_ref[:, ACT_GS + c0:ACT_GS + c0 + PROJ_COLS] = (g * _sigmoid(g)).astype(BF16)
        return run

    def gate_task(c0):
        def run():
            gt = proj(OFF_GT + c0, PROJ_COLS) + bg_ref[:, c0:c0 + PROJ_COLS]
            act_ref[:, ACT_GT + c0:ACT_GT + c0 + PROJ_COLS] = _sigmoid(gt).astype(BF16)
        return run

    tasks = []
    for hd in range(RET_HEADS):
        tasks.append(rotary_task(OFF_Q, ACT_Q, hd, cos_q, sin_q))
        tasks.append(rotary_task(OFF_K, ACT_K, hd, cos, sin))
    tasks += [v_task(c0) for c0 in range(0, RET_V_W, PROJ_COLS)]
    tasks += [g_task(c0) for c0 in range(0, RET_V_W, PROJ_COLS)]
    tasks += [gate_task(c0) for c0 in range(0, 2 * D_MODEL, PROJ_COLS)]

    for c0 in range(0, CONV_CH, PROJ_COLS):
        a = proj(OFF_CA + c0, PROJ_COLS)
        b = proj(OFF_CB + c0, PROJ_COLS)
        cpad_ref[CONV_HALO:CONV_HALO + rows, c0:c0 + PROJ_COLS] = a * _sigmoid(b)

    first = CONV_HALO - (CONV_WIDTH - 1)
    lane_groups = CONV_CH // LANES
    per_group = -(-len(tasks) // lane_groups)
    for grp in range(lane_groups):
        l0 = grp * LANES
        for r0 in range(0, rows, CONV_ROW_CHUNK):
            acc = None
            for res in range(SUBLANES):
                z = None
                for j in range(CONV_WIDTH):
                    off = first + j
                    if off % SUBLANES != res:
                        continue
                    term = cw_ref[j:j + 1, l0:l0 + LANES] * cpad_ref[r0 + off:r0 + off + CONV_ROW_CHUNK,
                                                                     l0:l0 + LANES]
                    z = term if z is None else z + term
                if z is not None:
                    acc = z if acc is None else acc + z
            y_ref[r0:r0 + CONV_ROW_CHUNK, l0:l0 + LANES] = acc + cb_ref[:, l0:l0 + LANES]
        for task in tasks[grp * per_group:(grp + 1) * per_group]:
            task()

    cpad_ref[0:CONV_HALO, :] = cpad_ref[rows:rows + CONV_HALO, :]


def _inproj(h, layer, p, rope):
    seq = h.shape[0]
    t = INPROJ_ROWS
    tiles = seq // t
    lyr = lambda *rest: (layer,) + rest
    cur = lambda width: pl.BlockSpec((t, width), lambda i: (jnp.minimum(i, tiles - 1), 0))
    return pl.pallas_call(
        functools.partial(_inproj_body, layer=layer),
        out_shape=(
            jax.ShapeDtypeStruct((seq, ACT_W), BF16),
            jax.ShapeDtypeStruct((seq, CONV_CH), BF16),
        ),
        grid=(tiles + 1,),
        in_specs=[
            cur(D_MODEL),
            _resident((None, 1, D_MODEL), lyr(0, 0)),
            UNBLOCKED,
            cur(RET_QK_DIM),
            _resident((None, 1, 2 * D_MODEL), lyr(0, 0)),
            _resident((None, CONV_WIDTH, CONV_CH), lyr(0, 0)),
            _resident((None, 1, CONV_CH), lyr(0, 0)),
            _resident((None, 1, CONV_CH), lyr(0, 0)),
            _resident((None, 1, CONV_CH), lyr(0, 0)),
        ],
        out_specs=(
            cur(ACT_W),
            pl.BlockSpec((t, CONV_CH), lambda i: (jnp.maximum(i - 1, 0), 0)),
        ),
        scratch_shapes=[
            pltpu.VMEM((CONV_HALO + t, CONV_CH), F32),
            pltpu.VMEM((t, CONV_CH), F32),
            pltpu.VMEM((D_MODEL, IN_W), BF16),
            *_staging(IN_W, [D_MODEL]),
        ],
        compiler_params=_params("arbitrary"),
        name="inproj",
    )(h, p["norm_mix_g"], p["w_in"], rope, p["b_gate"], p["conv_dw_w"], p["conv_dw_b"],
      p["conv_ln_g"], p["conv_ln_b"])


def _retention_body(act_ref, h_ref, cs_ref, dm_ref, dq_ref, dk_ref, db_ref,
                    gn_ref, bc_ref, wo_any, wc_any, wm_any, out_ref,
                    state_ref, gated_ref, wo_ref, wc_ref, wm_ref, w_stage, w_sem, *, layer):
    @pl.when(pl.program_id(0) == 0)
    def _():
        state_ref[...] = jnp.zeros(state_ref.shape, F32)
        _stage_weight(wo_any, layer, wo_ref, w_stage, w_sem)
        _stage_weight(wc_any, layer, wc_ref, w_stage, w_sem)
        _stage_weight(wm_any, layer, wm_ref, w_stage, w_sem)

    for r0 in range(0, act_ref.shape[0], RET_BLOCK):
        blk = slice(r0, r0 + RET_BLOCK)
        for hd in range(RET_HEADS):
            vv = slice(hd * RET_V_DIM, (hd + 1) * RET_V_DIM)
            q = act_ref[blk, ACT_Q + hd * RET_QK_DIM:ACT_Q + (hd + 1) * RET_QK_DIM]
            k = act_ref[blk, ACT_K + hd * RET_QK_DIM:ACT_K + (hd + 1) * RET_QK_DIM]
            v = act_ref[blk, ACT_V + hd * RET_V_DIM:ACT_V + (hd + 1) * RET_V_DIM]
            gs = act_ref[blk, ACT_GS + hd * RET_V_DIM:ACT_GS + (hd + 1) * RET_V_DIM]
            s = _dot_nt(q, k) * dm_ref[hd]
            o = _dot(s.astype(BF16), v)
            st = state_ref[hd]
            cross = _dot(q, st.astype(BF16))
            o = o + cross * jnp.concatenate([dq_ref[hd]] * (RET_V_DIM // LANES), axis=-1)
            kd = k.astype(F32) * jnp.concatenate([dk_ref[hd]] * (RET_QK_DIM // LANES), axis=-1)
            state_ref[hd] = st * db_ref[hd] + _dot(kd.T.astype(BF16), v)
            mu = jnp.mean(o, axis=-1, keepdims=True)
            d = o - mu
            var = jnp.mean(d * d, axis=-1, keepdims=True)
            y = d * lax.rsqrt(var + LN_EPS) * gn_ref[:, vv]
            gated_ref[blk, vv] = (gs.astype(F32) * y).astype(BF16)

    ya = _dot(gated_ref[...], wo_ref[...])
    yb = _dot(cs_ref[...], wc_ref[...]) + bc_ref[...]
    ga = act_ref[:, ACT_GT:ACT_GT + D_MODEL].astype(F32)
    gb = act_ref[:, ACT_GT + D_MODEL:ACT_W].astype(F32)
    m = (ga * ya + gb * yb).astype(BF16)
    out_ref[...] = h_ref[...] + _dot(m, wm_ref[...])


def _retention(h, act, cs, layer, p, consts):
    seq = h.shape[0]
    t = RET_ROWS
    dmask, dq, dk, dblk = consts
    lyr = lambda *rest: (layer,) + rest
    return pl.pallas_call(
        functools.partial(_retention_body, layer=layer),
        out_shape=jax.ShapeDtypeStruct((seq, D_MODEL), F32),
        grid=(seq // t,),
        in_specs=[
            _rows(t, ACT_W), _rows(t, D_MODEL), _rows(t, CONV_CH),
            _resident(dmask.shape, (0, 0, 0)),
            _resident(dq.shape, (0, 0, 0)),
            _resident(dk.shape, (0, 0, 0)),
            _resident(dblk.shape, (0, 0, 0)),
            _resident((None, 1, RET_V_W), lyr(0, 0)),
            _resident((None, 1, D_MODEL), lyr(0, 0)),
            UNBLOCKED, UNBLOCKED, UNBLOCKED,
        ],
        out_specs=_rows(t, D_MODEL),
        scratch_shapes=[
            pltpu.VMEM((RET_HEADS, RET_QK_DIM, RET_V_DIM), F32),
            pltpu.VMEM((t, RET_V_W), BF16),
            pltpu.VMEM((RET_V_W, D_MODEL), BF16),
            pltpu.VMEM((CONV_CH, D_MODEL), BF16),
            pltpu.VMEM((D_MODEL, D_MODEL), BF16),
            *_staging(D_MODEL, [RET_V_W, CONV_CH, D_MODEL]),
        ],
        compiler_params=_params("arbitrary"),
        name="retention_mix",
    )(act, h, cs, dmask, dq, dk, dblk, p["ret_gn_g"], p["b_conv_out"],
      p["w_ret_out"], p["w_conv_out"], p["w_mix_out"])


def _xprep_body(mem_ref, g_ref, wkv_ref, wq_ref, wo_ref, a_ref, b_ref):
    mn = _rms_norm(mem_ref[...], g_ref[...]).astype(BF16)
    kv = _dot(mn, wkv_ref[...].astype(BF16))
    for hd in range(X_HEADS):
        cols = slice(hd * X_HEAD_DIM, (hd + 1) * X_HEAD_DIM)
        k_h = kv[:, cols].astype(BF16)
        v_h = kv[:, D_MODEL + hd * X_HEAD_DIM:D_MODEL + (hd + 1) * X_HEAD_DIM].astype(BF16)
        a_ref[:, cols] = (_dot_nt(wq_ref[:, cols].astype(BF16), k_h) * (X_HEAD_DIM ** -0.5)).astype(BF16)
        b_ref[cols, :] = _dot(v_h, wo_ref[cols, :].astype(BF16)).astype(BF16)


def _xprep(mem, norm_g, w_xkv, w_xq, w_xo):
    depth = w_xkv.shape[0]
    mlen = mem.shape[0]
    assert mlen == X_HEAD_DIM
    per_layer = lambda *shape: pl.BlockSpec((None,) + shape, lambda l: (l,) + (0,) * len(shape))
    return pl.pallas_call(
        _xprep_body,
        out_shape=(jax.ShapeDtypeStruct((depth, D_MODEL, X_HEADS * mlen), BF16),
                   jax.ShapeDtypeStruct((depth, X_HEADS * mlen, D_MODEL), BF16)),
        grid=(depth,),
        in_specs=[
            _resident((mlen, D_MODEL), (0, 0)),
            per_layer(1, D_MODEL),
            per_layer(D_MODEL, 2 * D_MODEL),
            per_layer(D_MODEL, D_MODEL),
            per_layer(D_MODEL, D_MODEL),
        ],
        out_specs=(per_layer(D_MODEL, X_HEADS * mlen), per_layer(X_HEADS * mlen, D_MODEL)),
        compiler_params=_params("parallel"),
        name="xprep",
    )(mem, norm_g, w_xkv, w_xq, w_xo)


def _xattn_body(h_ref, g_ref, a_ref, b_ref, out_ref, p_ref):
    x = h_ref[...]
    hn = _rms_norm(x, g_ref[...]).astype(BF16)
    s = _dot(hn, a_ref[...])
    mlen = s.shape[1] // X_HEADS
    for hd in range(X_HEADS):
        cols = slice(hd * mlen, (hd + 1) * mlen)
        sh = s[:, cols]
        e = jnp.exp(sh - jnp.max(sh, axis=-1, keepdims=True))
        p_ref[:, cols] = (e * (1.0 / jnp.sum(e, axis=-1, keepdims=True))).astype(BF16)
    out_ref[...] = x + _dot(p_ref[...], b_ref[...])


def _xattn(h, xa, xb, layer, p):
    seq = h.shape[0]
    t = XATTN_ROWS
    lyr = lambda *rest: (layer,) + rest
    return pl.pallas_call(
        _xattn_body,
        out_shape=jax.ShapeDtypeStruct((seq, D_MODEL), F32),
        grid=(seq // t,),
        in_specs=[
            _rows(t, D_MODEL),
            _resident((None, 1, D_MODEL), lyr(0, 0)),
            _resident((None,) + xa.shape[1:], lyr(0, 0)),
            _resident((None,) + xb.shape[1:], lyr(0, 0)),
        ],
        out_specs=_rows(t, D_MODEL),
        scratch_shapes=[pltpu.VMEM((t, xa.shape[2]), BF16)],
        compiler_params=_params("parallel"),
        name="xattn",
    )(h, p["norm_xattn_g"], xa, xb)


def _ffn_body(h_ref, g_ref, wu_any, fw_ref, fb_ref, wd_any, fg_ref, out_ref, act_ref, prev_ref,
              wu_ref, wu_stage, wu_sem, wd_ref, wd_stage, wd_sem, *, layer, final_norm):
    rows = h_ref.shape[0]

    @pl.when(pl.program_id(0) == 0)
    def _():
        prev_ref[...] = jnp.zeros(prev_ref.shape, F32)
        _stage_weight(wu_any, layer, wu_ref, wu_stage, wu_sem)
        _stage_weight(wd_any, layer, wd_ref, wd_stage, wd_sem)

    x = h_ref[...]
    hn = _rms_norm(x, g_ref[...]).astype(BF16)
    row = lax.broadcasted_iota(jnp.int32, (rows, FFN_COLS), 0)
    for c0 in range(0, FFN_DIM, FFN_COLS):
        cols = slice(c0, c0 + FFN_COLS)
        val = _dot(hn, wu_ref[:, cols])
        gate = _dot(hn, wu_ref[:, FFN_DIM + c0:FFN_DIM + c0 + FFN_COLS])
        prev = prev_ref[:, cols]
        p1 = prev[SUBLANES - 1:SUBLANES, :]
        p2 = prev[SUBLANES - 2:SUBLANES - 1, :]
        g1 = jnp.where(row == 0, p1, pltpu.roll(gate, 1, axis=0))
        g2 = jnp.where(row == 0, p2, jnp.where(row == 1, p1, pltpu.roll(gate, 2, axis=0)))
        conv = fw_ref[0:1, cols] * g2 + fw_ref[1:2, cols] * g1 + fw_ref[2:3, cols] * gate + fb_ref[:, cols]
        act_ref[:, cols] = (conv * _sigmoid(conv) * val).astype(BF16)
        prev_ref[:, cols] = gate[rows - SUBLANES:, :]
    out = x + _dot(act_ref[...], wd_ref[...])
    if final_norm:
        out = _rms_norm(out, fg_ref[...])
    out_ref[...] = out


def _ffn(h, layer, p, final_g, final_norm):
    seq = h.shape[0]
    t = FFN_ROWS
    lyr = lambda *rest: (layer,) + rest
    return pl.pallas_call(
        functools.partial(_ffn_body, layer=layer, final_norm=final_norm),
        out_shape=jax.ShapeDtypeStruct((seq, D_MODEL), F32),
        grid=(seq // t,),
        in_specs=[
            _rows(t, D_MODEL),
            _resident((None, 1, D_MODEL), lyr(0, 0)),
            UNBLOCKED,
            _resident((None, FFN_CONV_WIDTH, FFN_DIM), lyr(0, 0)),
            _resident((None, 1, FFN_DIM), lyr(0, 0)),
            UNBLOCKED,
            _resident((1, D_MODEL), (0, 0)),
        ],
        out_specs=_rows(t, D_MODEL),
        scratch_shapes=[
            pltpu.VMEM((t, FFN_DIM), BF16),
            pltpu.VMEM((SUBLANES, FFN_DIM), F32),
            pltpu.VMEM((D_MODEL, 2 * FFN_DIM), BF16), *_staging(2 * FFN_DIM, [D_MODEL]),
            pltpu.VMEM((FFN_DIM, D_MODEL), BF16), *_staging(D_MODEL, [FFN_DIM]),
        ],
        compiler_params=_params("arbitrary"),
        name="ffn",
    )(h, p["norm_ffn_g"], p["w_up"], p["ffn_dw_w"], p["ffn_dw_b"], p["w_down"], final_g)


def _retention_constants(block):
    log_gamma = jnp.log(1.0 - jnp.power(2.0, -5.0 - jnp.arange(RET_HEADS, dtype=F32)))
    idx = jnp.arange(block, dtype=jnp.int32)
    n = idx[:, None]
    m = idx[None, :]
    same = (n // CHUNK) == (m // CHUNK)
    earlier = (m // CHUNK) < (n // CHUNK)
    dist = jnp.where(same, jnp.abs(n - m), n - m).astype(F32)
    dmask = jnp.where((same | earlier)[None], jnp.exp(log_gamma[:, None, None] * dist[None]), 0.0)
    r = jnp.arange(block, dtype=F32)
    dq = jnp.exp(log_gamma[:, None] * (r[None, :] + 1.0))
    dk = jnp.exp(log_gamma[:, None] * (block - 1.0 - r[None, :]))
    dq = jnp.broadcast_to(dq[:, :, None], (RET_HEADS, block, LANES))
    dk = jnp.broadcast_to(dk[:, :, None], (RET_HEADS, block, LANES))
    dblk = jnp.broadcast_to(jnp.exp(log_gamma * block)[:, None, None], (RET_HEADS, 1, RET_V_DIM))
    return dmask.astype(F32), dq, dk, dblk


def kernel(x, mem, positions, norm_mix_g, w_in, b_gate, ret_gn_g, w_ret_out, conv_dw_w, conv_dw_b,
           conv_ln_g, conv_ln_b, w_conv_out, b_conv_out, w_mix_out, norm_xattn_g, norm_mem_g, w_xq,
           w_xkv, w_xo, norm_ffn_g, w_up, ffn_dw_w, ffn_dw_b, w_down, norm_final_g):
    batch, seq, d_model = x.shape
    depth = w_in.shape[0]
    assert batch == 1 and d_model == D_MODEL and w_in.shape[2] == IN_W

    vec = lambda a: a.reshape(a.shape[0], 1, a.shape[1])
    p = {
        "norm_mix_g": vec(norm_mix_g), "w_in": w_in, "b_gate": vec(b_gate),
        "ret_gn_g": vec(ret_gn_g), "w_ret_out": w_ret_out,
        "conv_dw_w": conv_dw_w, "conv_dw_b": vec(conv_dw_b),
        "conv_ln_g": vec(conv_ln_g), "conv_ln_b": vec(conv_ln_b),
        "w_conv_out": w_conv_out, "b_conv_out": vec(b_conv_out),
        "w_mix_out": w_mix_out,
        "norm_xattn_g": vec(norm_xattn_g),
        "norm_ffn_g": vec(norm_ffn_g), "w_up": w_up,
        "ffn_dw_w": ffn_dw_w, "ffn_dw_b": vec(ffn_dw_b), "w_down": w_down,
    }

    inv_freq = 1.0 / (ROPE_THETA ** (jnp.arange(0, RET_QK_DIM, 2, dtype=F32) / RET_QK_DIM))
    rope = _rope_tables(positions.astype(F32).reshape(seq, 1), inv_freq.reshape(1, -1))
    consts = _retention_constants(RET_BLOCK)
    xa, xb = _xprep(mem[0], vec(norm_mem_g), w_xkv, w_xq, w_xo)
    final_g = norm_final_g.reshape(1, d_model)

    h = x[0]
    for layer in range(depth):
        act, cs = _inproj(h, layer, p, rope)
        h = _retention(h, act, cs, layer, p, consts)
        h = _xattn(h, xa, xb, layer, p)
        h = _ffn(h, layer, p, final_g, final_norm=(layer == depth - 1))
    return h[None]
```

```python
import functools

import jax
import jax.numpy as jnp
from jax import lax
from jax.experimental import pallas as pl
from jax.experimental.pallas import tpu as pltpu

F32 = jnp.float32
BF16 = jnp.bfloat16

D_MODEL = 1024
CHUNK = 64
RET_HEADS = 4
RET_QK_DIM = 256
RET_V_DIM = 512
RET_QK_W = RET_HEADS * RET_QK_DIM
RET_V_W = RET_HEADS * RET_V_DIM
ROPE_THETA = 10000.0
CONV_CH = D_MODEL
CONV_WIDTH = 31
X_HEADS = 4
X_HEAD_DIM = D_MODEL // X_HEADS
FFN_DIM = 2816
FFN_CONV_WIDTH = 3
RMS_EPS = 1e-6
LN_EPS = 1e-5
NEG_LOG2_E = -1.4426950408889634

OFF_Q = 0
OFF_K = OFF_Q + RET_QK_W
OFF_V = OFF_K + RET_QK_W
OFF_G = OFF_V + RET_V_W
OFF_CA = OFF_G + RET_V_W
OFF_CB = OFF_CA + CONV_CH
OFF_GT = OFF_CB + CONV_CH
IN_W = OFF_GT + 2 * D_MODEL

LANES = 128
SUBLANES = 8
BF16_ROWS = 16
VMEM_LIMIT = 56 * 1024 * 1024

ROPE_ROWS = 1024
INPROJ_ROWS = 256
RET_BLOCK = 256
RET_ROWS = 512
XATTN_ROWS = 1024
FFN_ROWS = 512
FFN_COLS = 256
CONV_HALO = 32
CONV_ROW_CHUNK = 64
PROJ_COLS = 256
STAGE_SLOT_BYTES = 11 * 512 * 1024
STAGE_SLOTS = 2


def _resident(block_shape, index):
    return pl.BlockSpec(block_shape, lambda i: index, pipeline_mode=pl.Buffered(1))


def _rows(block_rows, width):
    return pl.BlockSpec((block_rows, width), lambda i: (i, 0))


def _params(semantics):
    return pltpu.CompilerParams(dimension_semantics=(semantics,), vmem_limit_bytes=VMEM_LIMIT)


UNBLOCKED = pl.BlockSpec(memory_space=pl.ANY)


def _stage_rows(n, ks):
    fits = [r for r in range(BF16_ROWS, min(ks) + 1, BF16_ROWS)
            if all(k % r == 0 for k in ks) and r * n * 4 <= STAGE_SLOT_BYTES]
    return max(fits)


def _staging(n, ks):
    return [pltpu.VMEM((STAGE_SLOTS, _stage_rows(n, ks), n), F32), pltpu.SemaphoreType.DMA((STAGE_SLOTS,))]


def _stage_weight(w_any, layer, w_vmem, stage, sem):
    slots, rows, _ = stage.shape
    chunks = w_vmem.shape[0] // rows

    def copy(c):
        return pltpu.make_async_copy(w_any.at[layer, pl.ds(c * rows, rows), :], stage.at[c % slots],
                                     sem.at[c % slots])

    for c in range(min(slots, chunks)):
        copy(c).start(priority=c % slots)
    for c in range(chunks):
        copy(c).wait()
        w_vmem[c * rows:(c + 1) * rows, :] = stage[c % slots].astype(BF16)
        if c + slots < chunks:
            copy(c + slots).start(priority=c % slots)


def _rms_norm(x, g):
    ms = jnp.mean(x * x, axis=-1, keepdims=True)
    return x * lax.rsqrt(ms + RMS_EPS) * g


def _sigmoid(x):
    return 1.0 / (1.0 + jnp.exp2(x * NEG_LOG2_E))


def _dot(a, b):
    return jnp.dot(a, b, preferred_element_type=F32)


def _dot_nt(a, b):
    return lax.dot_general(a, b, (((1,), (1,)), ((), ())), preferred_element_type=F32)


def _rope_body(pos_ref, invf_ref, cos_ref, sin_ref):
    ang = pos_ref[...] * invf_ref[...]
    cos_ref[...] = jnp.cos(ang)
    sin_ref[...] = jnp.sin(ang)


def _rope_tables(pos, inv_freq):
    seq = pos.shape[0]
    half = inv_freq.shape[1]
    return pl.pallas_call(
        _rope_body,
        out_shape=(jax.ShapeDtypeStruct((seq, half), F32),) * 2,
        grid=(seq // ROPE_ROWS,),
        in_specs=[_rows(ROPE_ROWS, 1), _resident((1, half), (0, 0))],
        out_specs=(_rows(ROPE_ROWS, half),) * 2,
        compiler_params=_params("parallel"),
        name="rope_tables",
    )(pos, inv_freq)


def _inproj_body(h_ref, ng_ref, w_any, cos_ref, sin_ref, bg_ref, cw_ref, cb_ref, lg_ref, lb_ref,
                 q_ref, k_ref, v_ref, gs_ref, cs_ref, gt_ref, cpad_ref, y_ref, w_ref, w_stage, w_sem,
                 *, layer):
    rows = h_ref.shape[0]
    half = RET_QK_DIM // 2

    @pl.when(pl.program_id(0) == 0)
    def _():
        cpad_ref[0:CONV_HALO, :] = jnp.zeros((CONV_HALO, CONV_CH), F32)
        y_ref[...] = jnp.zeros(y_ref.shape, F32)
        _stage_weight(w_any, layer, w_ref, w_stage, w_sem)

    y = y_ref[...]
    mu = jnp.mean(y, axis=-1, keepdims=True)
    d = y - mu
    var = jnp.mean(d * d, axis=-1, keepdims=True)
    z = d * lax.rsqrt(var + LN_EPS) * lg_ref[...] + lb_ref[...]
    cs_ref[...] = (z * _sigmoid(z)).astype(BF16)

    u = _rms_norm(h_ref[...], ng_ref[...]).astype(BF16)
    cos = cos_ref[...]
    sin = sin_ref[...]
    q_scale = RET_QK_DIM ** -0.5
    cos_q = cos * q_scale
    sin_q = sin * q_scale

    def proj(col, width):
        return _dot(u, w_ref[:, col:col + width])

    def rotary_task(base, out_ref, hd, cs_, sn_):
        def run():
            p = proj(base + hd * RET_QK_DIM, RET_QK_DIM)
            x1 = p[:, :half]
            x2 = p[:, half:]
            c0 = hd * RET_QK_DIM
            out_ref[:, c0:c0 + half] = (x1 * cs_ - x2 * sn_).astype(BF16)
            out_ref[:, c0 + half:c0 + RET_QK_DIM] = (x2 * cs_ + x1 * sn_).astype(BF16)
        return run

    def v_task(c0):
        def run():
            v_ref[:, c0:c0 + PROJ_COLS] = proj(OFF_V + c0, PROJ_COLS).astype(BF16)
        return run

    def g_task(c0):
        def run():
            g = proj(OFF_G + c0, PROJ_COLS)
            gs_ref[:, c0:c0 + PROJ_COLS] = (g * _sigmoid(g)).astype(BF16)
        return run

    def gate_task(c0):
        def run():
            gt = proj(OFF_GT + c0, PROJ_COLS) + bg_ref[:, c0:c0 + PROJ_COLS]
            gt_ref[:, c0:c0 + PROJ_COLS] = _sigmoid(gt).astype(BF16)
        return run

    tasks = []
    for hd in range(RET_HEADS):
        tasks.append(rotary_task(OFF_Q, q_ref, hd, cos_q, sin_q))
        tasks.append(rotary_task(OFF_K, k_ref, hd, cos, sin))
    tasks += [v_task(c0) for c0 in range(0, RET_V_W, PROJ_COLS)]
    tasks += [g_task(c0) for c0 in range(0, RET_V_W, PROJ_COLS)]
    tasks += [gate_task(c0) for c0 in range(0, 2 * D_MODEL, PROJ_COLS)]

    for c0 in range(0, CONV_CH, PROJ_COLS):
        a = proj(OFF_CA + c0, PROJ_COLS)
        b = proj(OFF_CB + c0, PROJ_COLS)
        cpad_ref[CONV_HALO:CONV_HALO + rows, c0:c0 + PROJ_COLS] = a * _sigmoid(b)

    first = CONV_HALO - (CONV_WIDTH - 1)
    lane_groups = CONV_CH // LANES
    per_group = -(-len(tasks) // lane_groups)
    for grp in range(lane_groups):
        l0 = grp * LANES
        for r0 in range(0, rows, CONV_ROW_CHUNK):
            acc = None
            for res in range(SUBLANES):
                offs = [first + j for j in range(CONV_WIDTH) if (first + j) % SUBLANES == res]
                win = cpad_ref[r0 + offs[0]:r0 + offs[-1] + CONV_ROW_CHUNK, l0:l0 + LANES]
                z = None
                for off in offs:
                    j = off - first
                    k = off - offs[0]
                    term = cw_ref[j:j + 1, l0:l0 + LANES] * win[k:k + CONV_ROW_CHUNK]
                    z = term if z is None else z + term
                acc = z if acc is None else acc + z
            y_ref[r0:r0 + CONV_ROW_CHUNK, l0:l0 + LANES] = acc + cb_ref[:, l0:l0 + LANES]
        for task in tasks[grp * per_group:(grp + 1) * per_group]:
            task()

    cpad_ref[0:CONV_HALO, :] = cpad_ref[rows:rows + CONV_HALO, :]


def _inproj(h, layer, p, cos, sin):
    seq = h.shape[0]
    t = INPROJ_ROWS
    tiles = seq // t
    lyr = lambda *rest: (layer,) + rest
    cur = lambda width: pl.BlockSpec((t, width), lambda i: (jnp.minimum(i, tiles - 1), 0))
    return pl.pallas_call(
        functools.partial(_inproj_body, layer=layer),
        out_shape=(
            jax.ShapeDtypeStruct((seq, RET_QK_W), BF16),
            jax.ShapeDtypeStruct((seq, RET_QK_W), BF16),
            jax.ShapeDtypeStruct((seq, RET_V_W), BF16),
            jax.ShapeDtypeStruct((seq, RET_V_W), BF16),
            jax.ShapeDtypeStruct((seq, CONV_CH), BF16),
            jax.ShapeDtypeStruct((seq, 2 * D_MODEL), BF16),
        ),
        grid=(tiles + 1,),
        in_specs=[
            cur(D_MODEL),
            _resident((None, 1, D_MODEL), lyr(0, 0)),
            UNBLOCKED,
            cur(RET_QK_DIM // 2),
            cur(RET_QK_DIM // 2),
            _resident((None, 1, 2 * D_MODEL), lyr(0, 0)),
            _resident((None, CONV_WIDTH, CONV_CH), lyr(0, 0)),
            _resident((None, 1, CONV_CH), lyr(0, 0)),
            _resident((None, 1, CONV_CH), lyr(0, 0)),
            _resident((None, 1, CONV_CH), lyr(0, 0)),
        ],
        out_specs=(
            cur(RET_QK_W), cur(RET_QK_W), cur(RET_V_W), cur(RET_V_W),
            pl.BlockSpec((t, CONV_CH), lambda i: (jnp.maximum(i - 1, 0), 0)),
            cur(2 * D_MODEL),
        ),
        scratch_shapes=[
            pltpu.VMEM((CONV_HALO + t, CONV_CH), F32),
            pltpu.VMEM((t, CONV_CH), F32),
            pltpu.VMEM((D_MODEL, IN_W), BF16),
            *_staging(IN_W, [D_MODEL]),
        ],
        compiler_params=_params("arbitrary"),
        name="inproj",
    )(h, p["norm_mix_g"], p["w_in"], cos, sin, p["b_gate"], p["conv_dw_w"], p["conv_dw_b"],
      p["conv_ln_g"], p["conv_ln_b"])


def _retention_body(q_ref, k_ref, v_ref, gs_ref, h_ref, cs_ref, gt_ref, dm_ref, dq_ref, dk_ref, db_ref,
                    gn_ref, bc_ref, wo_any, wc_any, wm_any, out_ref,
                    state_ref, gated_ref, wo_ref, wc_ref, wm_ref, w_stage, w_sem, *, layer):
    @pl.when(pl.program_id(0) == 0)
    def _():
        state_ref[...] = jnp.zeros(state_ref.shape, F32)
        _stage_weight(wo_any, layer, wo_ref, w_stage, w_sem)
        _stage_weight(wc_any, layer, wc_ref, w_stage, w_sem)
        _stage_weight(wm_any, layer, wm_ref, w_stage, w_sem)

    for r0 in range(0, q_ref.shape[0], RET_BLOCK):
        blk = slice(r0, r0 + RET_BLOCK)
        for hd in range(RET_HEADS):
            qk = slice(hd * RET_QK_DIM, (hd + 1) * RET_QK_DIM)
            vv = slice(hd * RET_V_DIM, (hd + 1) * RET_V_DIM)
            q = q_ref[blk, qk]
            k = k_ref[blk, qk]
            v = v_ref[blk, vv]
            s = _dot_nt(q, k) * dm_ref[hd]
            o = _dot(s.astype(BF16), v)
            st = state_ref[hd]
            cross = _dot(q, st.astype(BF16))
            o = o + cross * jnp.concatenate([dq_ref[hd]] * (RET_V_DIM // LANES), axis=-1)
            kd = k.astype(F32) * jnp.concatenate([dk_ref[hd]] * (RET_QK_DIM // LANES), axis=-1)
            state_ref[hd] = st * db_ref[hd] + _dot(kd.T.astype(BF16), v)
            mu = jnp.mean(o, axis=-1, keepdims=True)
            d = o - mu
            var = jnp.mean(d * d, axis=-1, keepdims=True)
            y = d * lax.rsqrt(var + LN_EPS) * gn_ref[:, vv]
            gated_ref[blk, vv] = (gs_ref[blk, vv].astype(F32) * y).astype(BF16)

    ya = _dot(gated_ref[...], wo_ref[...])
    yb = _dot(cs_ref[...], wc_ref[...]) + bc_ref[...]
    ga = gt_ref[:, :D_MODEL].astype(F32)
    gb = gt_ref[:, D_MODEL:].astype(F32)
    m = (ga * ya + gb * yb).astype(BF16)
    out_ref[...] = h_ref[...] + _dot(m, wm_ref[...])


def _retention(h, q, k, v, gs, cs, gt, layer, p, consts):
    seq = q.shape[0]
    t = RET_ROWS
    dmask, dq, dk, dblk = consts
    lyr = lambda *rest: (layer,) + rest
    return pl.pallas_call(
        functools.partial(_retention_body, layer=layer),
        out_shape=jax.ShapeDtypeStruct((seq, D_MODEL), F32),
        grid=(seq // t,),
        in_specs=[
            _rows(t, RET_QK_W), _rows(t, RET_QK_W), _rows(t, RET_V_W), _rows(t, RET_V_W),
            _rows(t, D_MODEL), _rows(t, CONV_CH), _rows(t, 2 * D_MODEL),
            _resident(dmask.shape, (0, 0, 0)),
            _resident(dq.shape, (0, 0, 0)),
            _resident(dk.shape, (0, 0, 0)),
            _resident(dblk.shape, (0, 0, 0)),
            _resident((None, 1, RET_V_W), lyr(0, 0)),
            _resident((None, 1, D_MODEL), lyr(0, 0)),
            UNBLOCKED, UNBLOCKED, UNBLOCKED,
        ],
        out_specs=_rows(t, D_MODEL),
        scratch_shapes=[
            pltpu.VMEM((RET_HEADS, RET_QK_DIM, RET_V_DIM), F32),
            pltpu.VMEM((t, RET_V_W), BF16),
            pltpu.VMEM((RET_V_W, D_MODEL), BF16),
            pltpu.VMEM((CONV_CH, D_MODEL), BF16),
            pltpu.VMEM((D_MODEL, D_MODEL), BF16),
            *_staging(D_MODEL, [RET_V_W, CONV_CH, D_MODEL]),
        ],
        compiler_params=_params("arbitrary"),
        name="retention_mix",
    )(q, k, v, gs, h, cs, gt, dmask, dq, dk, dblk, p["ret_gn_g"], p["b_conv_out"],
      p["w_ret_out"], p["w_conv_out"], p["w_mix_out"])


def _xprep_body(mem_ref, g_ref, wkv_ref, wq_ref, wo_ref, a_ref, b_ref):
    mn = _rms_norm(mem_ref[...], g_ref[...]).astype(BF16)
    kv = _dot(mn, wkv_ref[...].astype(BF16))
    for hd in range(X_HEADS):
        cols = slice(hd * X_HEAD_DIM, (hd + 1) * X_HEAD_DIM)
        k_h = kv[:, cols].astype(BF16)
        v_h = kv[:, D_MODEL + hd * X_HEAD_DIM:D_MODEL + (hd + 1) * X_HEAD_DIM].astype(BF16)
        a_ref[:, cols] = (_dot_nt(wq_ref[:, cols].astype(BF16), k_h) * (X_HEAD_DIM ** -0.5)).astype(BF16)
        b_ref[cols, :] = _dot(v_h, wo_ref[cols, :].astype(BF16)).astype(BF16)


def _xprep(mem, norm_g, w_xkv, w_xq, w_xo):
    depth = w_xkv.shape[0]
    mlen = mem.shape[0]
    assert mlen == X_HEAD_DIM
    per_layer = lambda *shape: pl.BlockSpec((None,) + shape, lambda l: (l,) + (0,) * len(shape))
    return pl.pallas_call(
        _xprep_body,
        out_shape=(jax.ShapeDtypeStruct((depth, D_MODEL, X_HEADS * mlen), BF16),
                   jax.ShapeDtypeStruct((depth, X_HEADS * mlen, D_MODEL), BF16)),
        grid=(depth,),
        in_specs=[
            _resident((mlen, D_MODEL), (0, 0)),
            per_layer(1, D_MODEL),
            per_layer(D_MODEL, 2 * D_MODEL),
            per_layer(D_MODEL, D_MODEL),
            per_layer(D_MODEL, D_MODEL),
        ],
        out_specs=(per_layer(D_MODEL, X_HEADS * mlen), per_layer(X_HEADS * mlen, D_MODEL)),
        compiler_params=_params("parallel"),
        name="xprep",
    )(mem, norm_g, w_xkv, w_xq, w_xo)


def _xattn_body(h_ref, g_ref, a_ref, b_ref, out_ref, p_ref):
    x = h_ref[...]
    hn = _rms_norm(x, g_ref[...]).astype(BF16)
    s = _dot(hn, a_ref[...])
    mlen = s.shape[1] // X_HEADS
    for hd in range(X_HEADS):
        cols = slice(hd * mlen, (hd + 1) * mlen)
        sh = s[:, cols]
        e = jnp.exp(sh - jnp.max(sh, axis=-1, keepdims=True))
        p_ref[:, cols] = (e * (1.0 / jnp.sum(e, axis=-1, keepdims=True))).astype(BF16)
    out_ref[...] = x + _dot(p_ref[...], b_ref[...])


def _xattn(h, xa, xb, layer, p):
    seq = h.shape[0]
    t = XATTN_ROWS
    lyr = lambda *rest: (layer,) + rest
    return pl.pallas_call(
        _xattn_body,
        out_shape=jax.ShapeDtypeStruct((seq, D_MODEL), F32),
        grid=(seq // t,),
        in_specs=[
            _rows(t, D_MODEL),
            _resident((None, 1, D_MODEL), lyr(0, 0)),
            _resident((None,) + xa.shape[1:], lyr(0, 0)),
            _resident((None,) + xb.shape[1:], lyr(0, 0)),
        ],
        out_specs=_rows(t, D_MODEL),
        scratch_shapes=[pltpu.VMEM((t, xa.shape[2]), BF16)],
        compiler_params=_params("parallel"),
        name="xattn",
    )(h, p["norm_xattn_g"], xa, xb)


def _ffn_body(h_ref, g_ref, wu_any, fw_ref, fb_ref, wd_any, fg_ref, out_ref, act_ref, prev_ref,
              wu_ref, wu_stage, wu_sem, wd_ref, wd_stage, wd_sem, *, layer, final_norm):
    rows = h_ref.shape[0]

    @pl.when(pl.program_id(0) == 0)
    def _():
        prev_ref[...] = jnp.zeros(prev_ref.shape, F32)
        _stage_weight(wu_any, layer, wu_ref, wu_stage, wu_sem)
        _stage_weight(wd_any, layer, wd_ref, wd_stage, wd_sem)

    x = h_ref[...]
    hn = _rms_norm(x, g_ref[...]).astype(BF16)
    row = lax.broadcasted_iota(jnp.int32, (rows, FFN_COLS), 0)
    for c0 in range(0, FFN_DIM, FFN_COLS):
        cols = slice(c0, c0 + FFN_COLS)
        val = _dot(hn, wu_ref[:, cols])
        gate = _dot(hn, wu_ref[:, FFN_DIM + c0:FFN_DIM + c0 + FFN_COLS])
        prev = prev_ref[:, cols]
        p1 = prev[SUBLANES - 1:SUBLANES, :]
        p2 = prev[SUBLANES - 2:SUBLANES - 1, :]
        g1 = jnp.where(row == 0, p1, pltpu.roll(gate, 1, axis=0))
        g2 = jnp.where(row == 0, p2, jnp.where(row == 1, p1, pltpu.roll(gate, 2, axis=0)))
        conv = fw_ref[0:1, cols] * g2 + fw_ref[1:2, cols] * g1 + fw_ref[2:3, cols] * gate + fb_ref[:, cols]
        act_ref[:, cols] = (conv * _sigmoid(conv) * val).astype(BF16)
        prev_ref[:, cols] = gate[rows - SUBLANES:, :]
    out = x + _dot(act_ref[...], wd_ref[...])
    if final_norm:
        out = _rms_norm(out, fg_ref[...])
    out_ref[...] = out


def _ffn(h, layer, p, final_g, final_norm):
    seq = h.shape[0]
    t = FFN_ROWS
    lyr = lambda *rest: (layer,) + rest
    return pl.pallas_call(
        functools.partial(_ffn_body, layer=layer, final_norm=final_norm),
        out_shape=jax.ShapeDtypeStruct((seq, D_MODEL), F32),
        grid=(seq // t,),
        in_specs=[
            _rows(t, D_MODEL),
            _resident((None, 1, D_MODEL), lyr(0, 0)),
            UNBLOCKED,
            _resident((None, FFN_CONV_WIDTH, FFN_DIM), lyr(0, 0)),
            _resident((None, 1, FFN_DIM), lyr(0, 0)),
            UNBLOCKED,
            _resident((1, D_MODEL), (0, 0)),
        ],
        out_specs=_rows(t, D_MODEL),
        scratch_shapes=[
            pltpu.VMEM((t, FFN_DIM), BF16),
            pltpu.VMEM((SUBLANES, FFN_DIM), F32),
            pltpu.VMEM((D_MODEL, 2 * FFN_DIM), BF16), *_staging(2 * FFN_DIM, [D_MODEL]),
            pltpu.VMEM((FFN_DIM, D_MODEL), BF16), *_staging(D_MODEL, [FFN_DIM]),
        ],
        compiler_params=_params("arbitrary"),
        name="ffn",
    )(h, p["norm_ffn_g"], p["w_up"], p["ffn_dw_w"], p["ffn_dw_b"], p["w_down"], final_g)


def _retention_constants(block):
    log_gamma = jnp.log(1.0 - jnp.power(2.0, -5.0 - jnp.arange(RET_HEADS, dtype=F32)))
    idx = jnp.arange(block, dtype=jnp.int32)
    n = idx[:, None]
    m = idx[None, :]
    same = (n // CHUNK) == (m // CHUNK)
    earlier = (m // CHUNK) < (n // CHUNK)
    dist = jnp.where(same, jnp.abs(n - m), n - m).astype(F32)
    dmask = jnp.where((same | earlier)[None], jnp.exp(log_gamma[:, None, None] * dist[None]), 0.0)
    r = jnp.arange(block, dtype=F32)
    dq = jnp.exp(log_gamma[:, None] * (r[None, :] + 1.0))
    dk = jnp.exp(log_gamma[:, None] * (block - 1.0 - r[None, :]))
    dq = jnp.broadcast_to(dq[:, :, None], (RET_HEADS, block, LANES))
    dk = jnp.broadcast_to(dk[:, :, None], (RET_HEADS, block, LANES))
    dblk = jnp.broadcast_to(jnp.exp(log_gamma * block)[:, None, None], (RET_HEADS, 1, RET_V_DIM))
    return dmask.astype(F32), dq, dk, dblk


def kernel(x, mem, positions, norm_mix_g, w_in, b_gate, ret_gn_g, w_ret_out, conv_dw_w, conv_dw_b,
           conv_ln_g, conv_ln_b, w_conv_out, b_conv_out, w_mix_out, norm_xattn_g, norm_mem_g, w_xq,
           w_xkv, w_xo, norm_ffn_g, w_up, ffn_dw_w, ffn_dw_b, w_down, norm_final_g):
    batch, seq, d_model = x.shape
    depth = w_in.shape[0]
    assert batch == 1 and d_model == D_MODEL and w_in.shape[2] == IN_W

    vec = lambda a: a.reshape(a.shape[0], 1, a.shape[1])
    p = {
        "norm_mix_g": vec(norm_mix_g), "w_in": w_in, "b_gate": vec(b_gate),
        "ret_gn_g": vec(ret_gn_g), "w_ret_out": w_ret_out,
        "conv_dw_w": conv_dw_w, "conv_dw_b": vec(conv_dw_b),
        "conv_ln_g": vec(conv_ln_g), "conv_ln_b": vec(conv_ln_b),
        "w_conv_out": w_conv_out, "b_conv_out": vec(b_conv_out),
        "w_mix_out": w_mix_out,
        "norm_xattn_g": vec(norm_xattn_g),
        "norm_ffn_g": vec(norm_ffn_g), "w_up": w_up,
        "ffn_dw_w": ffn_dw_w, "ffn_dw_b": vec(ffn_dw_b), "w_down": w_down,
    }

    inv_freq = 1.0 / (ROPE_THETA ** (jnp.arange(0, RET_QK_DIM, 2, dtype=F32) / RET_QK_DIM))
    cos, sin = _rope_tables(positions.astype(F32).reshape(seq, 1), inv_freq.reshape(1, -1))
    consts = _retention_constants(RET_BLOCK)
    xa, xb = _xprep(mem[0], vec(norm_mem_g), w_xkv, w_xq, w_xo)
    final_g = norm_final_g.reshape(1, d_model)

    h = x[0]
    for layer in range(depth):
        q, k, v, gs, cs, gt = _inproj(h, layer, p, cos, sin)
        h = _retention(h, q, k, v, gs, cs, gt, layer, p, consts)
        h = _xattn(h, xa, xb, layer, p)
        h = _ffn(h, layer, p, final_g, final_norm=(layer == depth - 1))
    return h[None]
```

```python
import functools

import jax
import jax.numpy as jnp
from jax import lax
from jax.experimental import pallas as pl
from jax.experimental.pallas import tpu as pltpu

F32 = jnp.float32
BF16 = jnp.bfloat16

D_MODEL = 1024
CHUNK = 64
RET_HEADS = 4
RET_QK_DIM = 256
RET_V_DIM = 512
RET_QK_W = RET_HEADS * RET_QK_DIM
RET_V_W = RET_HEADS * RET_V_DIM
ROPE_THETA = 10000.0
CONV_CH = D_MODEL
CONV_WIDTH = 31
X_HEADS = 4
X_HEAD_DIM = D_MODEL // X_HEADS
FFN_DIM = 2816
FFN_CONV_WIDTH = 3
RMS_EPS = 1e-6
LN_EPS = 1e-5
NEG_LOG2_E = -1.4426950408889634

OFF_Q = 0
OFF_K = OFF_Q + RET_QK_W
OFF_V = OFF_K + RET_QK_W
OFF_G = OFF_V + RET_V_W
OFF_CA = OFF_G + RET_V_W
OFF_CB = OFF_CA + CONV_CH
OFF_GT = OFF_CB + CONV_CH
IN_W = OFF_GT + 2 * D_MODEL

LANES = 128
SUBLANES = 8
BF16_ROWS = 16
VMEM_LIMIT = 56 * 1024 * 1024

ROPE_ROWS = 1024
INPROJ_ROWS = 256
RET_BLOCK = 256
RET_ROWS = 512
XATTN_ROWS = 1024
FFN_ROWS = 512
FFN_COLS = 256
CONV_HALO = 32
CONV_ROW_CHUNK = 64
PROJ_COLS = 256
STAGE_SLOT_BYTES = 11 * 512 * 1024
STAGE_SLOTS = 2


def _resident(block_shape, index):
    return pl.BlockSpec(block_shape, lambda i: index, pipeline_mode=pl.Buffered(1))


def _rows(block_rows, width):
    return pl.BlockSpec((block_rows, width), lambda i: (i, 0))


def _params(semantics):
    return pltpu.CompilerParams(dimension_semantics=(semantics,), vmem_limit_bytes=VMEM_LIMIT)


UNBLOCKED = pl.BlockSpec(memory_space=pl.ANY)


def _stage_rows(n, ks):
    fits = [r for r in range(BF16_ROWS, min(ks) + 1, BF16_ROWS)
            if all(k % r == 0 for k in ks) and r * n * 4 <= STAGE_SLOT_BYTES]
    return max(fits)


def _staging(n, ks):
    return [pltpu.VMEM((STAGE_SLOTS, _stage_rows(n, ks), n), F32), pltpu.SemaphoreType.DMA((STAGE_SLOTS,))]


def _stage_weight(w_any, layer, w_vmem, stage, sem):
    slots, rows, _ = stage.shape
    chunks = w_vmem.shape[0] // rows

    def copy(c):
        return pltpu.make_async_copy(w_any.at[layer, pl.ds(c * rows, rows), :], stage.at[c % slots],
                                     sem.at[c % slots])

    for c in range(min(slots, chunks)):
        copy(c).start(priority=c % slots)
    for c in range(chunks):
        copy(c).wait()
        w_vmem[c * rows:(c + 1) * rows, :] = stage[c % slots].astype(BF16)
        if c + slots < chunks:
            copy(c + slots).start(priority=c % slots)


def _rms_norm(x, g):
    ms = jnp.mean(x * x, axis=-1, keepdims=True)
    return x * lax.rsqrt(ms + RMS_EPS) * g


def _sigmoid(x):
    return 1.0 / (1.0 + jnp.exp2(x * NEG_LOG2_E))


def _dot(a, b):
    return jnp.dot(a, b, preferred_element_type=F32)


def _dot_nt(a, b):
    return lax.dot_general(a, b, (((1,), (1,)), ((), ())), preferred_element_type=F32)


def _rope_body(pos_ref, invf_ref, cos_ref, sin_ref):
    ang = pos_ref[...] * invf_ref[...]
    cos_ref[...] = jnp.cos(ang)
    sin_ref[...] = jnp.sin(ang)


def _rope_tables(pos, inv_freq):
    seq = pos.shape[0]
    half = inv_freq.shape[1]
    return pl.pallas_call(
        _rope_body,
        out_shape=(jax.ShapeDtypeStruct((seq, half), F32),) * 2,
        grid=(seq // ROPE_ROWS,),
        in_specs=[_rows(ROPE_ROWS, 1), _resident((1, half), (0, 0))],
        out_specs=(_rows(ROPE_ROWS, half),) * 2,
        compiler_params=_params("parallel"),
        name="rope_tables",
    )(pos, inv_freq)


def _inproj_norm_previous(y_ref, lg_ref, lb_ref, cs_ref):
    y = y_ref[...]
    mu = jnp.mean(y, axis=-1, keepdims=True)
    d = y - mu
    var = jnp.mean(d * d, axis=-1, keepdims=True)
    z = d * lax.rsqrt(var + LN_EPS) * lg_ref[...] + lb_ref[...]
    cs_ref[...] = (z * _sigmoid(z)).astype(BF16)


def _inproj_tile(h_ref, ng_ref, cos_ref, sin_ref, bg_ref, cw_ref, cb_ref,
                 q_ref, k_ref, v_ref, gs_ref, gt_ref, cpad_ref, y_ref, w_ref):
    rows = h_ref.shape[0]
    half = RET_QK_DIM // 2

    u = _rms_norm(h_ref[...], ng_ref[...]).astype(BF16)
    cos = cos_ref[...]
    sin = sin_ref[...]
    q_scale = RET_QK_DIM ** -0.5
    cos_q = cos * q_scale
    sin_q = sin * q_scale

    def proj(col, width):
        return _dot(u, w_ref[:, col:col + width])

    def rotary_task(base, out_ref, hd, cs_, sn_):
        def run():
            p = proj(base + hd * RET_QK_DIM, RET_QK_DIM)
            x1 = p[:, :half]
            x2 = p[:, half:]
            c0 = hd * RET_QK_DIM
            out_ref[:, c0:c0 + half] = (x1 * cs_ - x2 * sn_).astype(BF16)
            out_ref[:, c0 + half:c0 + RET_QK_DIM] = (x2 * cs_ + x1 * sn_).astype(BF16)
        return run

    def v_task(c0):
        def run():
            v_ref[:, c0:c0 + PROJ_COLS] = proj(OFF_V + c0, PROJ_COLS).astype(BF16)
        return run

    def g_task(c0):
        def run():
            g = proj(OFF_G + c0, PROJ_COLS)
            gs_ref[:, c0:c0 + PROJ_COLS] = (g * _sigmoid(g)).astype(BF16)
        return run

    def gate_task(c0):
        def run():
            gt = proj(OFF_GT + c0, PROJ_COLS) + bg_ref[:, c0:c0 + PROJ_COLS]
            gt_ref[:, c0:c0 + PROJ_COLS] = _sigmoid(gt).astype(BF16)
        return run

    tasks = []
    for hd in range(RET_HEADS):
        tasks.append(rotary_task(OFF_Q, q_ref, hd, cos_q, sin_q))
        tasks.append(rotary_task(OFF_K, k_ref, hd, cos, sin))
    tasks += [v_task(c0) for c0 in range(0, RET_V_W, PROJ_COLS)]
    tasks += [g_task(c0) for c0 in range(0, RET_V_W, PROJ_COLS)]
    tasks += [gate_task(c0) for c0 in range(0, 2 * D_MODEL, PROJ_COLS)]

    for c0 in range(0, CONV_CH, PROJ_COLS):
        a = proj(OFF_CA + c0, PROJ_COLS)
        b = proj(OFF_CB + c0, PROJ_COLS)
        cpad_ref[CONV_HALO:CONV_HALO + rows, c0:c0 + PROJ_COLS] = a * _sigmoid(b)

    first = CONV_HALO - (CONV_WIDTH - 1)
    lane_groups = CONV_CH // LANES
    per_group = -(-len(tasks) // lane_groups)
    for grp in range(lane_groups):
        l0 = grp * LANES
        for r0 in range(0, rows, CONV_ROW_CHUNK):
            acc = None
            for res in range(SUBLANES):
                offs = [first + j for j in range(CONV_WIDTH) if (first + j) % SUBLANES == res]
                win = cpad_ref[r0 + offs[0]:r0 + offs[-1] + CONV_ROW_CHUNK, l0:l0 + LANES]
                z = None
                for off in offs:
                    j = off - first
                    k = off - offs[0]
                    term = cw_ref[j:j + 1, l0:l0 + LANES] * win[k:k + CONV_ROW_CHUNK]
                    z = term if z is None else z + term
                acc = z if acc is None else acc + z
            y_ref[r0:r0 + CONV_ROW_CHUNK, l0:l0 + LANES] = acc + cb_ref[:, l0:l0 + LANES]
        for task in tasks[grp * per_group:(grp + 1) * per_group]:
            task()

    cpad_ref[0:CONV_HALO, :] = cpad_ref[rows:rows + CONV_HALO, :]


def _inproj_body(h_ref, ng_ref, w_any, cos_ref, sin_ref, bg_ref, cw_ref, cb_ref, lg_ref, lb_ref,
                 q_ref, k_ref, v_ref, gs_ref, cs_ref, gt_ref, cpad_ref, y_ref, w_ref, w_stage, w_sem,
                 *, layer):
    step = pl.program_id(0)
    last = pl.num_programs(0) - 1

    @pl.when(step == 0)
    def _():
        cpad_ref[0:CONV_HALO, :] = jnp.zeros((CONV_HALO, CONV_CH), F32)
        y_ref[...] = jnp.zeros(y_ref.shape, F32)
        _stage_weight(w_any, layer, w_ref, w_stage, w_sem)

    @pl.when(step < last)
    def _():
        _inproj_norm_previous(y_ref, lg_ref, lb_ref, cs_ref)
        _inproj_tile(h_ref, ng_ref, cos_ref, sin_ref, bg_ref, cw_ref, cb_ref,
                     q_ref, k_ref, v_ref, gs_ref, gt_ref, cpad_ref, y_ref, w_ref)

    @pl.when(step == last)
    def _():
        _inproj_norm_previous(y_ref, lg_ref, lb_ref, cs_ref)


def _inproj(h, layer, p, cos, sin):
    seq = h.shape[0]
    t = INPROJ_ROWS
    tiles = seq // t
    lyr = lambda *rest: (layer,) + rest
    cur = lambda width: pl.BlockSpec((t, width), lambda i: (jnp.minimum(i, tiles - 1), 0))
    return pl.pallas_call(
        functools.partial(_inproj_body, layer=layer),
        out_shape=(
            jax.ShapeDtypeStruct((seq, RET_QK_W), BF16),
            jax.ShapeDtypeStruct((seq, RET_QK_W), BF16),
            jax.ShapeDtypeStruct((seq, RET_V_W), BF16),
            jax.ShapeDtypeStruct((seq, RET_V_W), BF16),
            jax.ShapeDtypeStruct((seq, CONV_CH), BF16),
            jax.ShapeDtypeStruct((seq, 2 * D_MODEL), BF16),
        ),
        grid=(tiles + 1,),
        in_specs=[
            cur(D_MODEL),
            _resident((None, 1, D_MODEL), lyr(0, 0)),
            UNBLOCKED,
            cur(RET_QK_DIM // 2),
            cur(RET_QK_DIM // 2),
            _resident((None, 1, 2 * D_MODEL), lyr(0, 0)),
            _resident((None, CONV_WIDTH, CONV_CH), lyr(0, 0)),
            _resident((None, 1, CONV_CH), lyr(0, 0)),
            _resident((None, 1, CONV_CH), lyr(0, 0)),
            _resident((None, 1, CONV_CH), lyr(0, 0)),
        ],
        out_specs=(
            cur(RET_QK_W), cur(RET_QK_W), cur(RET_V_W), cur(RET_V_W),
            pl.BlockSpec((t, CONV_CH), lambda i: (jnp.maximum(i - 1, 0), 0)),
            cur(2 * D_MODEL),
        ),
        scratch_shapes=[
            pltpu.VMEM((CONV_HALO + t, CONV_CH), F32),
            pltpu.VMEM((t, CONV_CH), F32),
            pltpu.VMEM((D_MODEL, IN_W), BF16),
            *_staging(IN_W, [D_MODEL]),
        ],
        compiler_params=_params("arbitrary"),
        name="inproj",
    )(h, p["norm_mix_g"], p["w_in"], cos, sin, p["b_gate"], p["conv_dw_w"], p["conv_dw_b"],
      p["conv_ln_g"], p["conv_ln_b"])


def _retention_body(q_ref, k_ref, v_ref, gs_ref, h_ref, cs_ref, gt_ref, dm_ref, dq_ref, dk_ref, db_ref,
                    gn_ref, bc_ref, wo_any, wc_any, wm_any, out_ref,
                    state_ref, gated_ref, wo_ref, wc_ref, wm_ref, w_stage, w_sem, *, layer):
    @pl.when(pl.program_id(0) == 0)
    def _():
        state_ref[...] = jnp.zeros(state_ref.shape, F32)
        _stage_weight(wo_any, layer, wo_ref, w_stage, w_sem)
        _stage_weight(wc_any, layer, wc_ref, w_stage, w_sem)
        _stage_weight(wm_any, layer, wm_ref, w_stage, w_sem)

    for r0 in range(0, q_ref.shape[0], RET_BLOCK):
        blk = slice(r0, r0 + RET_BLOCK)
        for hd in range(RET_HEADS):
            qk = slice(hd * RET_QK_DIM, (hd + 1) * RET_QK_DIM)
            vv = slice(hd * RET_V_DIM, (hd + 1) * RET_V_DIM)
            q = q_ref[blk, qk]
            k = k_ref[blk, qk]
            v = v_ref[blk, vv]
            s = _dot_nt(q, k) * dm_ref[hd]
            o = _dot(s.astype(BF16), v)
            st = state_ref[hd]
            cross = _dot(q, st.astype(BF16))
            o = o + cross * jnp.concatenate([dq_ref[hd]] * (RET_V_DIM // LANES), axis=-1)
            kd = k.astype(F32) * jnp.concatenate([dk_ref[hd]] * (RET_QK_DIM // LANES), axis=-1)
            state_ref[hd] = st * db_ref[hd] + _dot(kd.T.astype(BF16), v)
            mu = jnp.mean(o, axis=-1, keepdims=True)
            d = o - mu
            var = jnp.mean(d * d, axis=-1, keepdims=True)
            y = d * lax.rsqrt(var + LN_EPS) * gn_ref[:, vv]
            gated_ref[blk, vv] = (gs_ref[blk, vv].astype(F32) * y).astype(BF16)

    ya = _dot(gated_ref[...], wo_ref[...])
    yb = _dot(cs_ref[...], wc_ref[...]) + bc_ref[...]
    ga = gt_ref[:, :D_MODEL].astype(F32)
    gb = gt_ref[:, D_MODEL:].astype(F32)
    m = (ga * ya + gb * yb).astype(BF16)
    out_ref[...] = h_ref[...] + _dot(m, wm_ref[...])


def _retention(h, q, k, v, gs, cs, gt, layer, p, consts):
    seq = q.shape[0]
    t = RET_ROWS
    dmask, dq, dk, dblk = consts
    lyr = lambda *rest: (layer,) + rest
    return pl.pallas_call(
        functools.partial(_retention_body, layer=layer),
        out_shape=jax.ShapeDtypeStruct((seq, D_MODEL), F32),
        grid=(seq // t,),
        in_specs=[
            _rows(t, RET_QK_W), _rows(t, RET_QK_W), _rows(t, RET_V_W), _rows(t, RET_V_W),
            _rows(t, D_MODEL), _rows(t, CONV_CH), _rows(t, 2 * D_MODEL),
            _resident(dmask.shape, (0, 0, 0)),
            _resident(dq.shape, (0, 0, 0)),
            _resident(dk.shape, (0, 0, 0)),
            _resident(dblk.shape, (0, 0, 0)),
            _resident((None, 1, RET_V_W), lyr(0, 0)),
            _resident((None, 1, D_MODEL), lyr(0, 0)),
            UNBLOCKED, UNBLOCKED, UNBLOCKED,
        ],
        out_specs=_rows(t, D_MODEL),
        scratch_shapes=[
            pltpu.VMEM((RET_HEADS, RET_QK_DIM, RET_V_DIM), F32),
            pltpu.VMEM((t, RET_V_W), BF16),
            pltpu.VMEM((RET_V_W, D_MODEL), BF16),
            pltpu.VMEM((CONV_CH, D_MODEL), BF16),
            pltpu.VMEM((D_MODEL, D_MODEL), BF16),
            *_staging(D_MODEL, [RET_V_W, CONV_CH, D_MODEL]),
        ],
        compiler_params=_params("arbitrary"),
        name="retention_mix",
    )(q, k, v, gs, h, cs, gt, dmask, dq, dk, dblk, p["ret_gn_g"], p["b_conv_out"],
      p["w_ret_out"], p["w_conv_out"], p["w_mix_out"])


def _xprep_body(mem_ref, g_ref, wkv_ref, wq_ref, wo_ref, a_ref, b_ref):
    mn = _rms_norm(mem_ref[...], g_ref[...]).astype(BF16)
    kv = _dot(mn, wkv_ref[...].astype(BF16))
    for hd in range(X_HEADS):
        cols = slice(hd * X_HEAD_DIM, (hd + 1) * X_HEAD_DIM)
        k_h = kv[:, cols].astype(BF16)
        v_h = kv[:, D_MODEL + hd * X_HEAD_DIM:D_MODEL + (hd + 1) * X_HEAD_DIM].astype(BF16)
        a_ref[:, cols] = (_dot_nt(wq_ref[:, cols].astype(BF16), k_h) * (X_HEAD_DIM ** -0.5)).astype(BF16)
        b_ref[cols, :] = _dot(v_h, wo_ref[cols, :].astype(BF16)).astype(BF16)


def _xprep(mem, norm_g, w_xkv, w_xq, w_xo):
    depth = w_xkv.shape[0]
    mlen = mem.shape[0]
    assert mlen == X_HEAD_DIM
    per_layer = lambda *shape: pl.BlockSpec((None,) + shape, lambda l: (l,) + (0,) * len(shape))
    return pl.pallas_call(
        _xprep_body,
        out_shape=(jax.ShapeDtypeStruct((depth, D_MODEL, X_HEADS * mlen), BF16),
                   jax.ShapeDtypeStruct((depth, X_HEADS * mlen, D_MODEL), BF16)),
        grid=(depth,),
        in_specs=[
            _resident((mlen, D_MODEL), (0, 0)),
            per_layer(1, D_MODEL),
            per_layer(D_MODEL, 2 * D_MODEL),
            per_layer(D_MODEL, D_MODEL),
            per_layer(D_MODEL, D_MODEL),
        ],
        out_specs=(per_layer(D_MODEL, X_HEADS * mlen), per_layer(X_HEADS * mlen, D_MODEL)),
        compiler_params=_params("parallel"),
        name="xprep",
    )(mem, norm_g, w_xkv, w_xq, w_xo)


def _xattn_body(h_ref, g_ref, a_ref, b_ref, out_ref, p_ref):
    x = h_ref[...]
    hn = _rms_norm(x, g_ref[...]).astype(BF16)
    s = _dot(hn, a_ref[...])
    mlen = s.shape[1] // X_HEADS
    for hd in range(X_HEADS):
        cols = slice(hd * mlen, (hd + 1) * mlen)
        sh = s[:, cols]
        e = jnp.exp(sh - jnp.max(sh, axis=-1, keepdims=True))
        p_ref[:, cols] = (e * (1.0 / jnp.sum(e, axis=-1, keepdims=True))).astype(BF16)
    out_ref[...] = x + _dot(p_ref[...], b_ref[...])


def _xattn(h, xa, xb, layer, p):
    seq = h.shape[0]
    t = XATTN_ROWS
    lyr = lambda *rest: (layer,) + rest
    return pl.pallas_call(
        _xattn_body,
        out_shape=jax.ShapeDtypeStruct((seq, D_MODEL), F32),
        grid=(seq // t,),
        in_specs=[
            _rows(t, D_MODEL),
            _resident((None, 1, D_MODEL), lyr(0, 0)),
            _resident((None,) + xa.shape[1:], lyr(0, 0)),
            _resident((None,) + xb.shape[1:], lyr(0, 0)),
        ],
        out_specs=_rows(t, D_MODEL),
        scratch_shapes=[pltpu.VMEM((t, xa.shape[2]), BF16)],
        compiler_params=_params("parallel"),
        name="xattn",
    )(h, p["norm_xattn_g"], xa, xb)


def _ffn_body(h_ref, g_ref, wu_any, fw_ref, fb_ref, wd_any, fg_ref, out_ref, act_ref, prev_ref,
              wu_ref, wu_stage, wu_sem, wd_ref, wd_stage, wd_sem, *, layer, final_norm):
    rows = h_ref.shape[0]

    @pl.when(pl.program_id(0) == 0)
    def _():
        prev_ref[...] = jnp.zeros(prev_ref.shape, F32)
        _stage_weight(wu_any, layer, wu_ref, wu_stage, wu_sem)
        _stage_weight(wd_any, layer, wd_ref, wd_stage, wd_sem)

    x = h_ref[...]
    hn = _rms_norm(x, g_ref[...]).astype(BF16)
    row = lax.broadcasted_iota(jnp.int32, (rows, FFN_COLS), 0)
    for c0 in range(0, FFN_DIM, FFN_COLS):
        cols = slice(c0, c0 + FFN_COLS)
        val = _dot(hn, wu_ref[:, cols])
        gate = _dot(hn, wu_ref[:, FFN_DIM + c0:FFN_DIM + c0 + FFN_COLS])
        prev = prev_ref[:, cols]
        p1 = prev[SUBLANES - 1:SUBLANES, :]
        p2 = prev[SUBLANES - 2:SUBLANES - 1, :]
        g1 = jnp.where(row == 0, p1, pltpu.roll(gate, 1, axis=0))
        g2 = jnp.where(row == 0, p2, jnp.where(row == 1, p1, pltpu.roll(gate, 2, axis=0)))
        conv = fw_ref[0:1, cols] * g2 + fw_ref[1:2, cols] * g1 + fw_ref[2:3, cols] * gate + fb_ref[:, cols]
        act_ref[:, cols] = (conv * _sigmoid(conv) * val).astype(BF16)
        prev_ref[:, cols] = gate[rows - SUBLANES:, :]
    out = x + _dot(act_ref[...], wd_ref[...])
    if final_norm:
        out = _rms_norm(out, fg_ref[...])
    out_ref[...] = out


def _ffn(h, layer, p, final_g, final_norm):
    seq = h.shape[0]
    t = FFN_ROWS
    lyr = lambda *rest: (layer,) + rest
    return pl.pallas_call(
        functools.partial(_ffn_body, layer=layer, final_norm=final_norm),
        out_shape=jax.ShapeDtypeStruct((seq, D_MODEL), F32),
        grid=(seq // t,),
        in_specs=[
            _rows(t, D_MODEL),
            _resident((None, 1, D_MODEL), lyr(0, 0)),
            UNBLOCKED,
            _resident((None, FFN_CONV_WIDTH, FFN_DIM), lyr(0, 0)),
            _resident((None, 1, FFN_DIM), lyr(0, 0)),
            UNBLOCKED,
            _resident((1, D_MODEL), (0, 0)),
        ],
        out_specs=_rows(t, D_MODEL),
        scratch_shapes=[
            pltpu.VMEM((t, FFN_DIM), BF16),
            pltpu.VMEM((SUBLANES, FFN_DIM), F32),
            pltpu.VMEM((D_MODEL, 2 * FFN_DIM), BF16), *_staging(2 * FFN_DIM, [D_MODEL]),
            pltpu.VMEM((FFN_DIM, D_MODEL), BF16), *_staging(D_MODEL, [FFN_DIM]),
        ],
        compiler_params=_params("arbitrary"),
        name="ffn",
    )(h, p["norm_ffn_g"], p["w_up"], p["ffn_dw_w"], p["ffn_dw_b"], p["w_down"], final_g)


def _retention_constants(block):
    log_gamma = jnp.log(1.0 - jnp.power(2.0, -5.0 - jnp.arange(RET_HEADS, dtype=F32)))
    idx = jnp.arange(block, dtype=jnp.int32)
    n = idx[:, None]
    m = idx[None, :]
    same = (n // CHUNK) == (m // CHUNK)
    earlier = (m // CHUNK) < (n // CHUNK)
    dist = jnp.where(same, jnp.abs(n - m), n - m).astype(F32)
    dmask = jnp.where((same | earlier)[None], jnp.exp(log_gamma[:, None, None] * dist[None]), 0.0)
    r = jnp.arange(block, dtype=F32)
    dq = jnp.exp(log_gamma[:, None] * (r[None, :] + 1.0))
    dk = jnp.exp(log_gamma[:, None] * (block - 1.0 - r[None, :]))
    dq = jnp.broadcast_to(dq[:, :, None], (RET_HEADS, block, LANES))
    dk = jnp.broadcast_to(dk[:, :, None], (RET_HEADS, block, LANES))
    dblk = jnp.broadcast_to(jnp.exp(log_gamma * block)[:, None, None], (RET_HEADS, 1, RET_V_DIM))
    return dmask.astype(F32), dq, dk, dblk


def kernel(x, mem, positions, norm_mix_g, w_in, b_gate, ret_gn_g, w_ret_out, conv_dw_w, conv_dw_b,
           conv_ln_g, conv_ln_b, w_conv_out, b_conv_out, w_mix_out, norm_xattn_g, norm_mem_g, w_xq,
           w_xkv, w_xo, norm_ffn_g, w_up, ffn_dw_w, ffn_dw_b, w_down, norm_final_g):
    batch, seq, d_model = x.shape
    depth = w_in.shape[0]
    assert batch == 1 and d_model == D_MODEL and w_in.shape[2] == IN_W

    vec = lambda a: a.reshape(a.shape[0], 1, a.shape[1])
    p = {
        "norm_mix_g": vec(norm_mix_g), "w_in": w_in, "b_gate": vec(b_gate),
        "ret_gn_g": vec(ret_gn_g), "w_ret_out": w_ret_out,
        "conv_dw_w": conv_dw_w, "conv_dw_b": vec(conv_dw_b),
        "conv_ln_g": vec(conv_ln_g), "conv_ln_b": vec(conv_ln_b),
        "w_conv_out": w_conv_out, "b_conv_out": vec(b_conv_out),
        "w_mix_out": w_mix_out,
        "norm_xattn_g": vec(norm_xattn_g),
        "norm_ffn_g": vec(norm_ffn_g), "w_up": w_up,
        "ffn_dw_w": ffn_dw_w, "ffn_dw_b": vec(ffn_dw_b), "w_down": w_down,
    }

    inv_freq = 1.0 / (ROPE_THETA ** (jnp.arange(0, RET_QK_DIM, 2, dtype=F32) / RET_QK_DIM))
    cos, sin = _rope_tables(positions.astype(F32).reshape(seq, 1), inv_freq.reshape(1, -1))
    consts = _retention_constants(RET_BLOCK)
    xa, xb = _xprep(mem[0], vec(norm_mem_g), w_xkv, w_xq, w_xo)
    final_g = norm_final_g.reshape(1, d_model)

    h = x[0]
    for layer in range(depth):
        q, k, v, gs, cs, gt = _inproj(h, layer, p, cos, sin)
        h = _retention(h, q, k, v, gs, cs, gt, layer, p, consts)
        h = _xattn(h, xa, xb, layer, p)
        h = _ffn(h, layer, p, final_g, final_norm=(layer == depth - 1))
    return h[None]
```

```python
import functools

import jax
import jax.numpy as jnp
from jax import lax
from jax.experimental import pallas as pl
from jax.experimental.pallas import tpu as pltpu

F32 = jnp.float32
BF16 = jnp.bfloat16

D_MODEL = 1024
CHUNK = 64
RET_HEADS = 4
RET_QK_DIM = 256
RET_V_DIM = 512
RET_QK_W = RET_HEADS * RET_QK_DIM
RET_V_W = RET_HEADS * RET_V_DIM
ROPE_THETA = 10000.0
CONV_CH = D_MODEL
CONV_WIDTH = 31
X_HEADS = 4
X_HEAD_DIM = D_MODEL // X_HEADS
FFN_DIM = 2816
FFN_CONV_WIDTH = 3
RMS_EPS = 1e-6
LN_EPS = 1e-5
NEG_LOG2_E = -1.4426950408889634

OFF_Q = 0
OFF_K = OFF_Q + RET_QK_W
OFF_V = OFF_K + RET_QK_W
OFF_G = OFF_V + RET_V_W
OFF_CA = OFF_G + RET_V_W
OFF_CB = OFF_CA + CONV_CH
OFF_GT = OFF_CB + CONV_CH
IN_W = OFF_GT + 2 * D_MODEL

LANES = 128
SUBLANES = 8
BF16_ROWS = 16
VMEM_LIMIT = 56 * 1024 * 1024

ROPE_ROWS = 1024
INPROJ_ROWS = 256
RET_BLOCK = 256
RET_ROWS = 512
XATTN_ROWS = 1024
FFN_ROWS = 512
FFN_COLS = 256
CONV_HALO = 32
CONV_ROW_CHUNK = 64
PROJ_COLS = 256
STAGE_SLOT_BYTES = 11 * 512 * 1024
STAGE_SLOTS = 2


def _resident(block_shape, index):
    return pl.BlockSpec(block_shape, lambda i: index, pipeline_mode=pl.Buffered(1))


def _rows(block_rows, width):
    return pl.BlockSpec((block_rows, width), lambda i: (i, 0))


def _params(semantics):
    return pltpu.CompilerParams(dimension_semantics=(semantics,), vmem_limit_bytes=VMEM_LIMIT)


UNBLOCKED = pl.BlockSpec(memory_space=pl.ANY)


def _stage_rows(n, ks, slot_bytes):
    fits = [r for r in range(BF16_ROWS, min(ks) + 1, BF16_ROWS)
            if all(k % r == 0 for k in ks) and r * n * 4 <= slot_bytes]
    return max(fits)


def _staging(n, ks, slot_bytes=STAGE_SLOT_BYTES):
    return [pltpu.VMEM((STAGE_SLOTS, _stage_rows(n, ks, slot_bytes), n), F32),
            pltpu.SemaphoreType.DMA((STAGE_SLOTS,))]


def _stage_weight(w_any, layer, w_vmem, stage, sem):
    slots, rows, _ = stage.shape
    chunks = w_vmem.shape[0] // rows

    def copy(c):
        return pltpu.make_async_copy(w_any.at[layer, pl.ds(c * rows, rows), :], stage.at[c % slots],
                                     sem.at[c % slots])

    for c in range(min(slots, chunks)):
        copy(c).start(priority=c % slots)
    for c in range(chunks):
        copy(c).wait()
        w_vmem[c * rows:(c + 1) * rows, :] = stage[c % slots].astype(BF16)
        if c + slots < chunks:
            copy(c + slots).start(priority=c % slots)


def _rms_norm(x, g):
    ms = jnp.mean(x * x, axis=-1, keepdims=True)
    return x * lax.rsqrt(ms + RMS_EPS) * g


def _sigmoid(x):
    return 1.0 / (1.0 + jnp.exp2(x * NEG_LOG2_E))


def _dot(a, b):
    return jnp.dot(a, b, preferred_element_type=F32)


def _dot_nt(a, b):
    return lax.dot_general(a, b, (((1,), (1,)), ((), ())), preferred_element_type=F32)


def _rope_body(pos_ref, invf_ref, cos_ref, sin_ref):
    ang = pos_ref[...] * invf_ref[...]
    cos_ref[...] = jnp.cos(ang)
    sin_ref[...] = jnp.sin(ang)


def _rope_tables(pos, inv_freq):
    seq = pos.shape[0]
    half = inv_freq.shape[1]
    return pl.pallas_call(
        _rope_body,
        out_shape=(jax.ShapeDtypeStruct((seq, half), F32),) * 2,
        grid=(seq // ROPE_ROWS,),
        in_specs=[_rows(ROPE_ROWS, 1), _resident((1, half), (0, 0))],
        out_specs=(_rows(ROPE_ROWS, half),) * 2,
        compiler_params=_params("parallel"),
        name="rope_tables",
    )(pos, inv_freq)


def _inproj_norm_previous(y_ref, lg_ref, lb_ref, cs_ref):
    y = y_ref[...]
    mu = jnp.mean(y, axis=-1, keepdims=True)
    d = y - mu
    var = jnp.mean(d * d, axis=-1, keepdims=True)
    z = d * lax.rsqrt(var + LN_EPS) * lg_ref[...] + lb_ref[...]
    cs_ref[...] = (z * _sigmoid(z)).astype(BF16)


def _inproj_tile(h_ref, ng_ref, cos_ref, sin_ref, bg_ref, cw_ref, cb_ref,
                 q_ref, k_ref, v_ref, gs_ref, gt_ref, cpad_ref, y_ref, w_ref):
    rows = h_ref.shape[0]
    half = RET_QK_DIM // 2

    u = _rms_norm(h_ref[...], ng_ref[...]).astype(BF16)
    cos = cos_ref[...]
    sin = sin_ref[...]
    q_scale = RET_QK_DIM ** -0.5
    cos_q = cos * q_scale
    sin_q = sin * q_scale

    def proj(col, width):
        return _dot(u, w_ref[:, col:col + width])

    def rotary_task(base, out_ref, hd, cs_, sn_):
        def run():
            p = proj(base + hd * RET_QK_DIM, RET_QK_DIM)
            x1 = p[:, :half]
            x2 = p[:, half:]
            c0 = hd * RET_QK_DIM
            out_ref[:, c0:c0 + half] = (x1 * cs_ - x2 * sn_).astype(BF16)
            out_ref[:, c0 + half:c0 + RET_QK_DIM] = (x2 * cs_ + x1 * sn_).astype(BF16)
        return run

    def v_task(c0):
        def run():
            v_ref[:, c0:c0 + PROJ_COLS] = proj(OFF_V + c0, PROJ_COLS).astype(BF16)
        return run

    def g_task(c0):
        def run():
            g = proj(OFF_G + c0, PROJ_COLS)
            gs_ref[:, c0:c0 + PROJ_COLS] = (g * _sigmoid(g)).astype(BF16)
        return run

    def gate_task(c0):
        def run():
            gt = proj(OFF_GT + c0, PROJ_COLS) + bg_ref[:, c0:c0 + PROJ_COLS]
            gt_ref[:, c0:c0 + PROJ_COLS] = _sigmoid(gt).astype(BF16)
        return run

    tasks = []
    for hd in range(RET_HEADS):
        tasks.append(rotary_task(OFF_Q, q_ref, hd, cos_q, sin_q))
        tasks.append(rotary_task(OFF_K, k_ref, hd, cos, sin))
    tasks += [v_task(c0) for c0 in range(0, RET_V_W, PROJ_COLS)]
    tasks += [g_task(c0) for c0 in range(0, RET_V_W, PROJ_COLS)]
    tasks += [gate_task(c0) for c0 in range(0, 2 * D_MODEL, PROJ_COLS)]

    for c0 in range(0, CONV_CH, PROJ_COLS):
        a = proj(OFF_CA + c0, PROJ_COLS)
        b = proj(OFF_CB + c0, PROJ_COLS)
        cpad_ref[CONV_HALO:CONV_HALO + rows, c0:c0 + PROJ_COLS] = a * _sigmoid(b)

    first = CONV_HALO - (CONV_WIDTH - 1)
    lane_groups = CONV_CH // LANES
    per_group = -(-len(tasks) // lane_groups)
    for grp in range(lane_groups):
        l0 = grp * LANES
        for r0 in range(0, rows, CONV_ROW_CHUNK):
            acc = None
            for res in range(SUBLANES):
                offs = [first + j for j in range(CONV_WIDTH) if (first + j) % SUBLANES == res]
                win = cpad_ref[r0 + offs[0]:r0 + offs[-1] + CONV_ROW_CHUNK, l0:l0 + LANES]
                z = None
                for off in offs:
                    j = off - first
                    k = off - offs[0]
                    term = cw_ref[j:j + 1, l0:l0 + LANES] * win[k:k + CONV_ROW_CHUNK]
                    z = term if z is None else z + term
                acc = z if acc is None else acc + z
            y_ref[r0:r0 + CONV_ROW_CHUNK, l0:l0 + LANES] = acc + cb_ref[:, l0:l0 + LANES]
        for task in tasks[grp * per_group:(grp + 1) * per_group]:
            task()

    cpad_ref[0:CONV_HALO, :] = cpad_ref[rows:rows + CONV_HALO, :]


def _inproj_body(h_ref, ng_ref, w_any, cos_ref, sin_ref, bg_ref, cw_ref, cb_ref, lg_ref, lb_ref,
                 q_ref, k_ref, v_ref, gs_ref, cs_ref, gt_ref, cpad_ref, y_ref, w_ref, w_stage, w_sem,
                 *, layer):
    step = pl.program_id(0)
    last = pl.num_programs(0) - 1

    @pl.when(step == 0)
    def _():
        cpad_ref[0:CONV_HALO, :] = jnp.zeros((CONV_HALO, CONV_CH), F32)
        y_ref[...] = jnp.zeros(y_ref.shape, F32)
        _stage_weight(w_any, layer, w_ref, w_stage, w_sem)

    @pl.when(step < last)
    def _():
        _inproj_norm_previous(y_ref, lg_ref, lb_ref, cs_ref)
        _inproj_tile(h_ref, ng_ref, cos_ref, sin_ref, bg_ref, cw_ref, cb_ref,
                     q_ref, k_ref, v_ref, gs_ref, gt_ref, cpad_ref, y_ref, w_ref)

    @pl.when(step == last)
    def _():
        _inproj_norm_previous(y_ref, lg_ref, lb_ref, cs_ref)


def _inproj(h, layer, p, cos, sin):
    seq = h.shape[0]
    t = INPROJ_ROWS
    tiles = seq // t
    lyr = lambda *rest: (layer,) + rest
    cur = lambda width: pl.BlockSpec((t, width), lambda i: (jnp.minimum(i, tiles - 1), 0))
    return pl.pallas_call(
        functools.partial(_inproj_body, layer=layer),
        out_shape=(
            jax.ShapeDtypeStruct((seq, RET_QK_W), BF16),
            jax.ShapeDtypeStruct((seq, RET_QK_W), BF16),
            jax.ShapeDtypeStruct((seq, RET_V_W), BF16),
            jax.ShapeDtypeStruct((seq, RET_V_W), BF16),
            jax.ShapeDtypeStruct((seq, CONV_CH), BF16),
            jax.ShapeDtypeStruct((seq, 2 * D_MODEL), BF16),
        ),
        grid=(tiles + 1,),
        in_specs=[
            cur(D_MODEL),
            _resident((None, 1, D_MODEL), lyr(0, 0)),
            UNBLOCKED,
            cur(RET_QK_DIM // 2),
            cur(RET_QK_DIM // 2),
            _resident((None, 1, 2 * D_MODEL), lyr(0, 0)),
            _resident((None, CONV_WIDTH, CONV_CH), lyr(0, 0)),
            _resident((None, 1, CONV_CH), lyr(0, 0)),
            _resident((None, 1, CONV_CH), lyr(0, 0)),
            _resident((None, 1, CONV_CH), lyr(0, 0)),
        ],
        out_specs=(
            cur(RET_QK_W), cur(RET_QK_W), cur(RET_V_W), cur(RET_V_W),
            pl.BlockSpec((t, CONV_CH), lambda i: (jnp.maximum(i - 1, 0), 0)),
            cur(2 * D_MODEL),
        ),
        scratch_shapes=[
            pltpu.VMEM((CONV_HALO + t, CONV_CH), F32),
            pltpu.VMEM((t, CONV_CH), F32),
            pltpu.VMEM((D_MODEL, IN_W), BF16),
            *_staging(IN_W, [D_MODEL]),
        ],
        compiler_params=_params("arbitrary"),
        name="inproj",
    )(h, p["norm_mix_g"], p["w_in"], cos, sin, p["b_gate"], p["conv_dw_w"], p["conv_dw_b"],
      p["conv_ln_g"], p["conv_ln_b"])


def _retention_body(q_ref, k_ref, v_ref, gs_ref, h_ref, cs_ref, gt_ref, dm_ref, dq_ref, dk_ref, db_ref,
                    gn_ref, bc_ref, wo_any, wc_any, wm_any, out_ref,
                    state_ref, gated_ref, wo_ref, wc_ref, wm_ref, w_stage, w_sem, *, layer):
    @pl.when(pl.program_id(0) == 0)
    def _():
        state_ref[...] = jnp.zeros(state_ref.shape, F32)
        _stage_weight(wo_any, layer, wo_ref, w_stage, w_sem)
        _stage_weight(wc_any, layer, wc_ref, w_stage, w_sem)
        _stage_weight(wm_any, layer, wm_ref, w_stage, w_sem)

    for r0 in range(0, q_ref.shape[0], RET_BLOCK):
        blk = slice(r0, r0 + RET_BLOCK)
        for hd in range(RET_HEADS):
            qk = slice(hd * RET_QK_DIM, (hd + 1) * RET_QK_DIM)
            vv = slice(hd * RET_V_DIM, (hd + 1) * RET_V_DIM)
            q = q_ref[blk, qk]
            k = k_ref[blk, qk]
            v = v_ref[blk, vv]
            s = _dot_nt(q, k) * dm_ref[hd]
            o = _dot(s.astype(BF16), v)
            st = state_ref[hd]
            cross = _dot(q, st.astype(BF16))
            o = o + cross * jnp.concatenate([dq_ref[hd]] * (RET_V_DIM // LANES), axis=-1)
            kd = k.astype(F32) * jnp.concatenate([dk_ref[hd]] * (RET_QK_DIM // LANES), axis=-1)
            state_ref[hd] = st * db_ref[hd] + _dot(kd.T.astype(BF16), v)
            mu = jnp.mean(o, axis=-1, keepdims=True)
            d = o - mu
            var = jnp.mean(d * d, axis=-1, keepdims=True)
            y = d * lax.rsqrt(var + LN_EPS) * gn_ref[:, vv]
            gated_ref[blk, vv] = (gs_ref[blk, vv].astype(F32) * y).astype(BF16)

    ya = _dot(gated_ref[...], wo_ref[...])
    yb = _dot(cs_ref[...], wc_ref[...]) + bc_ref[...]
    ga = gt_ref[:, :D_MODEL].astype(F32)
    gb = gt_ref[:, D_MODEL:].astype(F32)
    m = (ga * ya + gb * yb).astype(BF16)
    out_ref[...] = h_ref[...] + _dot(m, wm_ref[...])


def _retention(h, q, k, v, gs, cs, gt, layer, p, consts):
    seq = q.shape[0]
    t = RET_ROWS
    dmask, dq, dk, dblk = consts
    lyr = lambda *rest: (layer,) + rest
    return pl.pallas_call(
        functools.partial(_retention_body, layer=layer),
        out_shape=jax.ShapeDtypeStruct((seq, D_MODEL), F32),
        grid=(seq // t,),
        in_specs=[
            _rows(t, RET_QK_W), _rows(t, RET_QK_W), _rows(t, RET_V_W), _rows(t, RET_V_W),
            _rows(t, D_MODEL), _rows(t, CONV_CH), _rows(t, 2 * D_MODEL),
            _resident(dmask.shape, (0, 0, 0)),
            _resident(dq.shape, (0, 0, 0)),
            _resident(dk.shape, (0, 0, 0)),
            _resident(dblk.shape, (0, 0, 0)),
            _resident((None, 1, RET_V_W), lyr(0, 0)),
            _resident((None, 1, D_MODEL), lyr(0, 0)),
            UNBLOCKED, UNBLOCKED, UNBLOCKED,
        ],
        out_specs=_rows(t, D_MODEL),
        scratch_shapes=[
            pltpu.VMEM((RET_HEADS, RET_QK_DIM, RET_V_DIM), F32),
            pltpu.VMEM((t, RET_V_W), BF16),
            pltpu.VMEM((RET_V_W, D_MODEL), BF16),
            pltpu.VMEM((CONV_CH, D_MODEL), BF16),
            pltpu.VMEM((D_MODEL, D_MODEL), BF16),
            *_staging(D_MODEL, [RET_V_W, CONV_CH, D_MODEL]),
        ],
        compiler_params=_params("arbitrary"),
        name="retention_mix",
    )(q, k, v, gs, h, cs, gt, dmask, dq, dk, dblk, p["ret_gn_g"], p["b_conv_out"],
      p["w_ret_out"], p["w_conv_out"], p["w_mix_out"])


def _xprep_body(mem_ref, g_ref, wkv_ref, wq_ref, wo_ref, a_ref, b_ref):
    mn = _rms_norm(mem_ref[...], g_ref[...]).astype(BF16)
    kv = _dot(mn, wkv_ref[...].astype(BF16))
    for hd in range(X_HEADS):
        cols = slice(hd * X_HEAD_DIM, (hd + 1) * X_HEAD_DIM)
        k_h = kv[:, cols].astype(BF16)
        v_h = kv[:, D_MODEL + hd * X_HEAD_DIM:D_MODEL + (hd + 1) * X_HEAD_DIM].astype(BF16)
        a_ref[:, cols] = (_dot_nt(wq_ref[:, cols].astype(BF16), k_h) * (X_HEAD_DIM ** -0.5)).astype(BF16)
        b_ref[cols, :] = _dot(v_h, wo_ref[cols, :].astype(BF16)).astype(BF16)


def _xprep(mem, norm_g, w_xkv, w_xq, w_xo):
    depth = w_xkv.shape[0]
    mlen = mem.shape[0]
    assert mlen == X_HEAD_DIM
    per_layer = lambda *shape: pl.BlockSpec((None,) + shape, lambda l: (l,) + (0,) * len(shape))
    return pl.pallas_call(
        _xprep_body,
        out_shape=(jax.ShapeDtypeStruct((depth, D_MODEL, X_HEADS * mlen), BF16),
                   jax.ShapeDtypeStruct((depth, X_HEADS * mlen, D_MODEL), BF16)),
        grid=(depth,),
        in_specs=[
            _resident((mlen, D_MODEL), (0, 0)),
            per_layer(1, D_MODEL),
            per_layer(D_MODEL, 2 * D_MODEL),
            per_layer(D_MODEL, D_MODEL),
            per_layer(D_MODEL, D_MODEL),
        ],
        out_specs=(per_layer(D_MODEL, X_HEADS * mlen), per_layer(X_HEADS * mlen, D_MODEL)),
        compiler_params=_params("parallel"),
        name="xprep",
    )(mem, norm_g, w_xkv, w_xq, w_xo)


def _xattn_body(h_ref, g_ref, a_ref, b_ref, out_ref, p_ref):
    x = h_ref[...]
    hn = _rms_norm(x, g_ref[...]).astype(BF16)
    s = _dot(hn, a_ref[...])
    mlen = s.shape[1] // X_HEADS
    for hd in range(X_HEADS):
        cols = slice(hd * mlen, (hd + 1) * mlen)
        sh = s[:, cols]
        e = jnp.exp(sh - jnp.max(sh, axis=-1, keepdims=True))
        p_ref[:, cols] = (e * (1.0 / jnp.sum(e, axis=-1, keepdims=True))).astype(BF16)
    out_ref[...] = x + _dot(p_ref[...], b_ref[...])


def _xattn(h, xa, xb, layer, p):
    seq = h.shape[0]
    t = XATTN_ROWS
    lyr = lambda *rest: (layer,) + rest
    return pl.pallas_call(
        _xattn_body,
        out_shape=jax.ShapeDtypeStruct((seq, D_MODEL), F32),
        grid=(seq // t,),
        in_specs=[
            _rows(t, D_MODEL),
            _resident((None, 1, D_MODEL), lyr(0, 0)),
            _resident((None,) + xa.shape[1:], lyr(0, 0)),
            _resident((None,) + xb.shape[1:], lyr(0, 0)),
        ],
        out_specs=_rows(t, D_MODEL),
        scratch_shapes=[pltpu.VMEM((t, xa.shape[2]), BF16)],
        compiler_params=_params("parallel"),
        name="xattn",
    )(h, p["norm_xattn_g"], xa, xb)


def _ffn_body(h_ref, xg_ref, xa_ref, xb_ref, g_ref, wu_any, fw_ref, fb_ref, wd_any, fg_ref, out_ref,
              act_ref, prev_ref, p_ref, wu_ref, wu_stage, wu_sem, wd_ref, wd_stage, wd_sem,
              *, layer, final_norm):
    rows = h_ref.shape[0]

    @pl.when(pl.program_id(0) == 0)
    def _():
        prev_ref[...] = jnp.zeros(prev_ref.shape, F32)
        _stage_weight(wu_any, layer, wu_ref, wu_stage, wu_sem)
        _stage_weight(wd_any, layer, wd_ref, wd_stage, wd_sem)

    x = h_ref[...]
    s = _dot(_rms_norm(x, xg_ref[...]).astype(BF16), xa_ref[...])
    mlen = s.shape[1] // X_HEADS
    for hd in range(X_HEADS):
        mcols = slice(hd * mlen, (hd + 1) * mlen)
        sh = s[:, mcols]
        e = jnp.exp(sh - jnp.max(sh, axis=-1, keepdims=True))
        p_ref[:, mcols] = (e * (1.0 / jnp.sum(e, axis=-1, keepdims=True))).astype(BF16)
    x = x + _dot(p_ref[...], xb_ref[...])
    hn = _rms_norm(x, g_ref[...]).astype(BF16)
    row = lax.broadcasted_iota(jnp.int32, (rows, FFN_COLS), 0)
    for c0 in range(0, FFN_DIM, FFN_COLS):
        cols = slice(c0, c0 + FFN_COLS)
        val = _dot(hn, wu_ref[:, cols])
        gate = _dot(hn, wu_ref[:, FFN_DIM + c0:FFN_DIM + c0 + FFN_COLS])
        prev = prev_ref[:, cols]
        p1 = prev[SUBLANES - 1:SUBLANES, :]
        p2 = prev[SUBLANES - 2:SUBLANES - 1, :]
        g1 = jnp.where(row == 0, p1, pltpu.roll(gate, 1, axis=0))
        g2 = jnp.where(row == 0, p2, jnp.where(row == 1, p1, pltpu.roll(gate, 2, axis=0)))
        conv = fw_ref[0:1, cols] * g2 + fw_ref[1:2, cols] * g1 + fw_ref[2:3, cols] * gate + fb_ref[:, cols]
        act_ref[:, cols] = (conv * _sigmoid(conv) * val).astype(BF16)
        prev_ref[:, cols] = gate[rows - SUBLANES:, :]
    out = x + _dot(act_ref[...], wd_ref[...])
    if final_norm:
        out = _rms_norm(out, fg_ref[...])
    out_ref[...] = out


def _ffn(h, xa, xb, layer, p, final_g, final_norm):
    seq = h.shape[0]
    t = FFN_ROWS
    lyr = lambda *rest: (layer,) + rest
    half_slot = STAGE_SLOT_BYTES // 2
    return pl.pallas_call(
        functools.partial(_ffn_body, layer=layer, final_norm=final_norm),
        out_shape=jax.ShapeDtypeStruct((seq, D_MODEL), F32),
        grid=(seq // t,),
        in_specs=[
            _rows(t, D_MODEL),
            _resident((None, 1, D_MODEL), lyr(0, 0)),
            _resident((None,) + xa.shape[1:], lyr(0, 0)),
            _resident((None,) + xb.shape[1:], lyr(0, 0)),
            _resident((None, 1, D_MODEL), lyr(0, 0)),
            UNBLOCKED,
            _resident((None, FFN_CONV_WIDTH, FFN_DIM), lyr(0, 0)),
            _resident((None, 1, FFN_DIM), lyr(0, 0)),
            UNBLOCKED,
            _resident((1, D_MODEL), (0, 0)),
        ],
        out_specs=_rows(t, D_MODEL),
        scratch_shapes=[
            pltpu.VMEM((t, FFN_DIM), BF16),
            pltpu.VMEM((SUBLANES, FFN_DIM), F32),
            pltpu.VMEM((t, xa.shape[2]), BF16),
            pltpu.VMEM((D_MODEL, 2 * FFN_DIM), BF16), *_staging(2 * FFN_DIM, [D_MODEL], half_slot),
            pltpu.VMEM((FFN_DIM, D_MODEL), BF16), *_staging(D_MODEL, [FFN_DIM], half_slot),
        ],
        compiler_params=_params("arbitrary"),
        name="xattn_ffn",
    )(h, p["norm_xattn_g"], xa, xb, p["norm_ffn_g"], p["w_up"], p["ffn_dw_w"], p["ffn_dw_b"], p["w_down"], final_g)


def _retention_constants(block):
    log_gamma = jnp.log(1.0 - jnp.power(2.0, -5.0 - jnp.arange(RET_HEADS, dtype=F32)))
    idx = jnp.arange(block, dtype=jnp.int32)
    n = idx[:, None]
    m = idx[None, :]
    same = (n // CHUNK) == (m // CHUNK)
    earlier = (m // CHUNK) < (n // CHUNK)
    dist = jnp.where(same, jnp.abs(n - m), n - m).astype(F32)
    dmask = jnp.where((same | earlier)[None], jnp.exp(log_gamma[:, None, None] * dist[None]), 0.0)
    r = jnp.arange(block, dtype=F32)
    dq = jnp.exp(log_gamma[:, None] * (r[None, :] + 1.0))
    dk = jnp.exp(log_gamma[:, None] * (block - 1.0 - r[None, :]))
    dq = jnp.broadcast_to(dq[:, :, None], (RET_HEADS, block, LANES))
    dk = jnp.broadcast_to(dk[:, :, None], (RET_HEADS, block, LANES))
    dblk = jnp.broadcast_to(jnp.exp(log_gamma * block)[:, None, None], (RET_HEADS, 1, RET_V_DIM))
    return dmask.astype(F32), dq, dk, dblk


def kernel(x, mem, positions, norm_mix_g, w_in, b_gate, ret_gn_g, w_ret_out, conv_dw_w, conv_dw_b,
           conv_ln_g, conv_ln_b, w_conv_out, b_conv_out, w_mix_out, norm_xattn_g, norm_mem_g, w_xq,
           w_xkv, w_xo, norm_ffn_g, w_up, ffn_dw_w, ffn_dw_b, w_down, norm_final_g):
    batch, seq, d_model = x.shape
    depth = w_in.shape[0]
    assert batch == 1 and d_model == D_MODEL and w_in.shape[2] == IN_W

    vec = lambda a: a.reshape(a.shape[0], 1, a.shape[1])
    p = {
        "norm_mix_g": vec(norm_mix_g), "w_in": w_in, "b_gate": vec(b_gate),
        "ret_gn_g": vec(ret_gn_g), "w_ret_out": w_ret_out,
        "conv_dw_w": conv_dw_w, "conv_dw_b": vec(conv_dw_b),
        "conv_ln_g": vec(conv_ln_g), "conv_ln_b": vec(conv_ln_b),
        "w_conv_out": w_conv_out, "b_conv_out": vec(b_conv_out),
        "w_mix_out": w_mix_out,
        "norm_xattn_g": vec(norm_xattn_g),
        "norm_ffn_g": vec(norm_ffn_g), "w_up": w_up,
        "ffn_dw_w": ffn_dw_w, "ffn_dw_b": vec(ffn_dw_b), "w_down": w_down,
    }

    inv_freq = 1.0 / (ROPE_THETA ** (jnp.arange(0, RET_QK_DIM, 2, dtype=F32) / RET_QK_DIM))
    cos, sin = _rope_tables(positions.astype(F32).reshape(seq, 1), inv_freq.reshape(1, -1))
    consts = _retention_constants(RET_BLOCK)
    xa, xb = _xprep(mem[0], vec(norm_mem_g), w_xkv, w_xq, w_xo)
    final_g = norm_final_g.reshape(1, d_model)

    h = x[0]
    for layer in range(depth):
        q, k, v, gs, cs, gt = _inproj(h, layer, p, cos, sin)
        h = _retention(h, q, k, v, gs, cs, gt, layer, p, consts)
        h = _ffn(h, xa, xb, layer, p, final_g, final_norm=(layer == depth - 1))
    return h[None]
```
